```python
import jax, jax.numpy as jnp
from jax import lax
import numpy as np

D_MODEL = 2048
BATCH = 2
SEQ = 4096
DEPTH = 1

CHUNK = 64
HEAD_DIM = 128
N_HEADS_A = 8
N_HEADS_B = 8
WIDTH_A = N_HEADS_A * HEAD_DIM
WIDTH_B = N_HEADS_B * HEAD_DIM
MIX_WIDTH = WIDTH_A + WIDTH_B
LEFT_CHUNKS = 8
BAND = (LEFT_CHUNKS + 1) * CHUNK
MAX_REL = 128
IDX_HEADS = 16
IDX_DIM = 64
TOPK_MAX = 256
Q_BLOCK = 128
ROPE_THETA = 500000.0
ROPE_FRACTION = 4
PEER_HEADS = 8
PEER_KEYS = 128
PEER_EXPERTS = PEER_KEYS * PEER_KEYS
PEER_QDIM = 256
PEER_TOPK = 16
PEER_BLOCK = 128
PLE_DIM = 256
IN_PROJ_WIDTH = 3 * WIDTH_A + 3 * WIDTH_B + IDX_HEADS * IDX_DIM + IDX_DIM + IDX_HEADS
EPS = 1e-6

kernel_name = "hybrid_chunk_relpos_dsa_peer_block"


def rms_norm(x, g):
    xf = x.astype(jnp.float32)
    y = xf * lax.rsqrt(jnp.mean(xf * xf, axis=-1, keepdims=True) + EPS)
    return (y * g.astype(jnp.float32)).astype(x.dtype)


def partial_rope(x, pos):
    rot = x.shape[-1] // ROPE_FRACTION
    half = rot // 2
    inv_freq = 1.0 / (ROPE_THETA ** (jnp.arange(half, dtype=jnp.float32) / half))
    ang = pos.astype(jnp.float32)[..., None] * inv_freq
    cos = jnp.cos(ang)[:, :, None, :]
    sin = jnp.sin(ang)[:, :, None, :]
    x1 = x[..., :half].astype(jnp.float32)
    x2 = x[..., half:rot].astype(jnp.float32)
    r1 = (x1 * cos - x2 * sin).astype(x.dtype)
    r2 = (x2 * cos + x1 * sin).astype(x.dtype)
    return jnp.concatenate([r1, r2, x[..., rot:]], axis=-1)


def split_offsets():
    sizes = (WIDTH_A, WIDTH_A, WIDTH_A, WIDTH_B, WIDTH_B, WIDTH_B,
             IDX_HEADS * IDX_DIM, IDX_DIM, IDX_HEADS)
    offs = []
    acc = 0
    for s in sizes[:-1]:
        acc += s
        offs.append(acc)
    return offs


def chunked_relpos_attention(q, k, v, rel_bias):
    B, S, H, d = q.shape
    nc = S // CHUNK
    qc = q.reshape(B, nc, CHUNK, H, d)
    pad = ((0, 0), (LEFT_CHUNKS * CHUNK, 0), (0, 0), (0, 0))
    kp = jnp.pad(k, pad).reshape(B, nc + LEFT_CHUNKS, CHUNK, H, d)
    vp = jnp.pad(v, pad).reshape(B, nc + LEFT_CHUNKS, CHUNK, H, d)
    band_idx = jnp.arange(nc)[:, None] + jnp.arange(LEFT_CHUNKS + 1)[None, :]
    kb = kp[:, band_idx].reshape(B, nc, BAND, H, d)
    vb = vp[:, band_idx].reshape(B, nc, BAND, H, d)
    s = jnp.einsum('bcqhd,bckhd->bhcqk', qc, kb, preferred_element_type=jnp.float32) * (d ** -0.5)
    dist = jnp.arange(CHUNK)[:, None] + LEFT_CHUNKS * CHUNK - jnp.arange(BAND)[None, :]
    dist = jnp.clip(dist, -MAX_REL, MAX_REL) + MAX_REL
    bias = rel_bias[:, dist].astype(jnp.float32)
    key_chunk = jnp.arange(nc)[:, None] - LEFT_CHUNKS + jnp.arange(LEFT_CHUNKS + 1)[None, :]
    valid = jnp.repeat(key_chunk >= 0, CHUNK, axis=1)
    s = s + bias[None, :, None]
    s = jnp.where(valid[None, None, :, None, :], s, -jnp.inf)
    pr = jax.nn.softmax(s, axis=-1)
    o = jnp.einsum('bhcqk,bckhd->bcqhd', pr.astype(vb.dtype), vb)
    return o.reshape(B, S, H * d)


def dsa_attention(q, k, v, q_idx, k_idx, w_idx):
    B, S, H, d = q.shape
    topk = min(TOPK_MAX, S // 4)
    nb = S // Q_BLOCK
    key_chunk = jnp.arange(S) // CHUNK

    def to_blocks(a):
        return a.reshape((B, nb, Q_BLOCK) + a.shape[2:]).swapaxes(0, 1)

    def block_fn(args):
        qb, qib, wb, start = args
        q_chunk = (start + jnp.arange(Q_BLOCK)) // CHUNK
        logits = jnp.einsum('bqhi,bsi->bqhs', qib, k_idx,
                            preferred_element_type=jnp.float32) * (IDX_DIM ** -0.5)
        score = jnp.einsum('bqh,bqhs->bqs', wb.astype(jnp.float32) * (IDX_HEADS ** -0.5),
                           jax.nn.relu(logits))
        admissible = key_chunk[None, :] <= q_chunk[:, None]
        score = jnp.where(admissible[None], score, -jnp.inf)
        _, sel = lax.top_k(score, topk)
        sel_ok = key_chunk[sel] <= q_chunk[None, :, None]
        ks = jax.vmap(lambda kk, ii: kk[ii])(k, sel)
        vs = jax.vmap(lambda vv, ii: vv[ii])(v, sel)
        att = jnp.einsum('bqhd,bqkhd->bqhk', qb, ks, preferred_element_type=jnp.float32) * (d ** -0.5)
        att = jnp.where(sel_ok[:, :, None, :], att, -jnp.inf)
        pr = jax.nn.softmax(att, axis=-1)
        return jnp.einsum('bqhk,bqkhd->bqhd', pr.astype(vs.dtype), vs)

    starts = jnp.arange(nb, dtype=jnp.int32) * Q_BLOCK
    out = lax.map(block_fn, (to_blocks(q), to_blocks(q_idx), to_blocks(w_idx), starts))
    return out.swapaxes(0, 1).reshape(B, S, H * d)


def peer_ffn(x, w_query, sub_keys, expert_u, expert_v):
    B, S, D = x.shape
    nb = S // PEER_BLOCK
    half = PEER_QDIM // 2
    xb = x.reshape(B, nb, PEER_BLOCK, D).swapaxes(0, 1)

    def block_fn(xt):
        qv = (xt @ w_query).reshape(B, PEER_BLOCK, PEER_HEADS, 2, half)
        s1 = jnp.einsum('bthd,nd->bthn', qv[..., 0, :], sub_keys[0], preferred_element_type=jnp.float32)
        s2 = jnp.einsum('bthd,nd->bthn', qv[..., 1, :], sub_keys[1], preferred_element_type=jnp.float32)
        v1, i1 = lax.top_k(s1, PEER_TOPK)
        v2, i2 = lax.top_k(s2, PEER_TOPK)
        cand = (v1[..., :, None] + v2[..., None, :]).reshape(B, PEER_BLOCK, PEER_HEADS, PEER_TOPK * PEER_TOPK)
        sc, ci = lax.top_k(cand, PEER_TOPK)
        e1 = jnp.take_along_axis(i1, ci // PEER_TOPK, axis=-1)
        e2 = jnp.take_along_axis(i2, ci % PEER_TOPK, axis=-1)
        expert = e1 * PEER_KEYS + e2
        g = jax.nn.softmax(sc, axis=-1)
        u_sel = expert_u[expert]
        v_sel = expert_v[expert]
        act = jax.nn.gelu(jnp.einsum('btd,bthkd->bthk', xt, u_sel, preferred_element_type=jnp.float32))
        coef = (g * act).astype(xt.dtype)
        return jnp.einsum('bthk,bthkd->btd', coef, v_sel)

    out = lax.map(block_fn, xb)
    return out.swapaxes(0, 1).reshape(B, S, D)


def setup_inputs(seed: int = 0) -> dict:
    key = jax.random.key(seed)
    ks = jax.random.split(key, 20)
    f32 = jnp.float32

    def nrm(k, shape, scale):
        return jax.random.normal(k, shape, f32) * scale

    def gain(k, shape):
        return 1.0 + 0.01 * jax.random.normal(k, shape, f32)

    x = jax.random.normal(ks[0], (BATCH, SEQ, D_MODEL), f32)
    p = jax.random.normal(ks[1], (DEPTH, BATCH, SEQ, PLE_DIM), f32)
    offset = jax.random.randint(ks[2], (BATCH, 1), 0, 1024, dtype=jnp.int32)
    positions = offset + jnp.arange(SEQ, dtype=jnp.int32)[None, :]
    return {
        "x": x,
        "p": p,
        "positions": positions,
        "norm_mix": gain(ks[3], (DEPTH, D_MODEL)),
        "w_in": nrm(ks[4], (DEPTH, D_MODEL, IN_PROJ_WIDTH), D_MODEL ** -0.5),
        "qk_norm_a": gain(ks[5], (DEPTH, 2, HEAD_DIM)),
        "qk_norm_b": gain(ks[6], (DEPTH, 2, HEAD_DIM)),
        "rel_bias": nrm(ks[7], (DEPTH, N_HEADS_A, 2 * MAX_REL + 1), 0.1),
        "w_out": nrm(ks[8], (DEPTH, MIX_WIDTH, D_MODEL), MIX_WIDTH ** -0.5),
        "norm_ffn": gain(ks[9], (DEPTH, D_MODEL)),
        "peer_query": nrm(ks[10], (DEPTH, D_MODEL, PEER_HEADS * PEER_QDIM), D_MODEL ** -0.5),
        "peer_sub_keys": nrm(ks[11], (DEPTH, 2, PEER_KEYS, PEER_QDIM // 2), (PEER_QDIM // 2) ** -0.5),
        "peer_u": nrm(ks[12], (DEPTH, PEER_EXPERTS, D_MODEL), D_MODEL ** -0.5),
        "peer_v": nrm(ks[13], (DEPTH, PEER_EXPERTS, D_MODEL), PEER_HEADS ** -0.5),
        "norm_ple": gain(ks[14], (DEPTH, D_MODEL)),
        "ple_gate": nrm(ks[15], (DEPTH, D_MODEL, D_MODEL), D_MODEL ** -0.5),
        "ple_proj": nrm(ks[16], (DEPTH, PLE_DIM, D_MODEL), PLE_DIM ** -0.5),
    }


def reference(x, p, positions, norm_mix, w_in, qk_norm_a, qk_norm_b, rel_bias, w_out,
              norm_ffn, peer_query, peer_sub_keys, peer_u, peer_v, norm_ple, ple_gate, ple_proj):
    B, S, _ = x.shape
    offs = split_offsets()
    h = x
    for i in range(DEPTH):
        hn = rms_norm(h, norm_mix[i])
        proj = hn @ w_in[i]
        qa, ka, va, qb, kb, vb, qi, ki, wi = jnp.split(proj, offs, axis=-1)
        qa = rms_norm(qa.reshape(B, S, N_HEADS_A, HEAD_DIM), qk_norm_a[i, 0])
        ka = rms_norm(ka.reshape(B, S, N_HEADS_A, HEAD_DIM), qk_norm_a[i, 1])
        va = va.reshape(B, S, N_HEADS_A, HEAD_DIM)
        out_a = chunked_relpos_attention(qa, ka, va, rel_bias[i])
        qb = partial_rope(rms_norm(qb.reshape(B, S, N_HEADS_B, HEAD_DIM), qk_norm_b[i, 0]), positions)
        kb = partial_rope(rms_norm(kb.reshape(B, S, N_HEADS_B, HEAD_DIM), qk_norm_b[i, 1]), positions)
        vb = vb.reshape(B, S, N_HEADS_B, HEAD_DIM)
        qi = partial_rope(qi.reshape(B, S, IDX_HEADS, IDX_DIM), positions)
        ki = partial_rope(ki[:, :, None, :], positions)[:, :, 0, :]
        out_b = dsa_attention(qb, kb, vb, qi, ki, wi)
        h = h + jnp.concatenate([out_a, out_b], axis=-1) @ w_out[i]
        h = h + peer_ffn(rms_norm(h, norm_ffn[i]), peer_query[i], peer_sub_keys[i], peer_u[i], peer_v[i])
        gate = jax.nn.sigmoid(rms_norm(h, norm_ple[i]) @ ple_gate[i])
        h = h + gate * (p[i] @ ple_proj[i])
    return h
```

```python
import functools

import numpy as np
import jax
import jax.numpy as jnp
from jax import lax
from jax.experimental import pallas as pl
from jax.experimental.pallas import tpu as pltpu

F32 = jnp.float32
BF16 = jnp.bfloat16

D_MODEL = 2048
HEAD_DIM = 128
N_HEADS = 8
WIDTH = N_HEADS * HEAD_DIM
CHUNK = 64
LEFT_CHUNKS = 8
MAX_REL = 128
IDX_HEADS = 16
IDX_DIM = 64
TOPK_MAX = 256
ROPE_THETA = 500000.0
ROPE_FRACTION = 4
PEER_HEADS = 8
PEER_KEYS = 128
PEER_TOPK = 16
PLE_DIM = 256
EPS = 1e-6

LANES = 128
SUBLANES = 8
VMEM_LIMIT = 56 * 1024 * 1024

NEG = -1e30

TM_PROJ = 512
QA_BLOCK = 256
QA_WIN = 768
TQ_DSA = 256
TM_ROUTE = 512
TM_EXP = 512
TE_EXP = 512
TM_FINAL = 256

_NT = (((1,), (1,)), ((), ()))


def _nt_dot(a, b):
    return lax.dot_general(a, b, _NT, preferred_element_type=F32)


def _rms(x, g):
    return x * lax.rsqrt(jnp.mean(x * x, axis=-1, keepdims=True) + EPS) * g


def _rope_lane_table(group, half):
    lane = np.arange(LANES)
    m = lane % group
    inv = (np.float32(1.0) / np.power(np.float32(ROPE_THETA),
                                      (np.arange(half, dtype=np.float32) / np.float32(half)))).astype(np.float32)
    tab = np.zeros((SUBLANES, LANES), np.float32)
    tab[0] = np.where(m < 2 * half, inv[m % half], 0.0)
    tab[1] = (m < half)
    tab[2] = (m >= half) & (m < 2 * half)
    return tab


def _rope_coeffs(pos_f, tab_ref, out_ref):
    ang = pos_f * tab_ref[0:1, :]
    s = jnp.sin(ang)
    out_ref[0] = jnp.cos(ang)
    out_ref[1] = -s * tab_ref[1:2, :]
    out_ref[2] = s * tab_ref[2:3, :]


def _apply_rope(x, coef_ref, half):
    up = pltpu.roll(x, LANES - half, 1)
    dn = pltpu.roll(x, half, 1)
    return x * coef_ref[0] + up * coef_ref[1] + dn * coef_ref[2]


def _relbias_kernel(u_ref, o_ref):
    x = jnp.broadcast_to(u_ref[0, 0:1, :], (QA_BLOCK, 1024))
    y = pltpu.roll(x, 0, 1, stride=1, stride_axis=0)[:, :QA_WIN]
    qc = lax.broadcasted_iota(jnp.int32, (QA_BLOCK, QA_WIN), 0) // CHUNK
    kc = lax.broadcasted_iota(jnp.int32, (QA_BLOCK, QA_WIN), 1) // CHUNK
    band = (kc >= qc) & (kc <= qc + LEFT_CHUNKS)
    o_ref[0] = jnp.where(band, y, NEG)


def _relbias_block(rel_bias):
    p = np.arange(1024)
    off = np.where(p < QA_WIN, p, p - 1024)
    idx = np.clip(LEFT_CHUNKS * CHUNK - off, -MAX_REL, MAX_REL) + MAX_REL
    u = jnp.take(rel_bias.astype(F32), jnp.asarray(idx, jnp.int32), axis=1)
    u = jnp.broadcast_to(u[:, None, :], (N_HEADS, SUBLANES, 1024))
    return pl.pallas_call(
        _relbias_kernel,
        grid=(N_HEADS,),
        in_specs=[pl.BlockSpec((1, SUBLANES, 1024), lambda h: (h, 0, 0))],
        out_specs=pl.BlockSpec((1, QA_BLOCK, QA_WIN), lambda h: (h, 0, 0)),
        out_shape=jax.ShapeDtypeStruct((N_HEADS, QA_BLOCK, QA_WIN), F32),
        name="relbias",
    )(u)


R_QA, R_KA, R_VA, R_QB, R_KB, R_QI = range(6)


def _inproj_kernel(x_ref, g_ref, w_ref, wvt_ref, wtail_ref, qkg_ref, pos_ref, tab128_ref, tab64_ref,
                   main_ref, vbt_ref, kiab_ref, wi_ref,
                   hn_ref, acc_ref, rope128_ref, rope64_ref):
    j = pl.program_id(1)
    tm = x_ref.shape[0]

    @pl.when(j == 0)
    def _():
        hn_ref[...] = _rms(x_ref[...], g_ref[...]).astype(BF16)
        pos_f = pos_ref[...].astype(F32)
        _rope_coeffs(pos_f, tab128_ref, rope128_ref)
        _rope_coeffs(pos_f, tab64_ref, rope64_ref)
        vt = _nt_dot(wvt_ref[...], hn_ref[...]).astype(BF16)
        for t in range(tm // TQ_DSA):
            vbt_ref[t] = vt[:, t * TQ_DSA:(t + 1) * TQ_DSA]
        tail = jnp.dot(hn_ref[...], wtail_ref[...], preferred_element_type=F32)
        kk = _apply_rope(tail[:, :LANES], rope64_ref, IDX_DIM // ROPE_FRACTION // 2)
        lane = lax.broadcasted_iota(jnp.int32, (tm, LANES), 1)
        kiab_ref[:, :LANES] = jnp.where(lane < IDX_DIM, kk, 0.0).astype(BF16)
        kiab_ref[:, LANES:] = jnp.where(lane >= IDX_DIM, kk, 0.0).astype(BF16)
        wi_ref[...] = tail[:, LANES:]

    acc_ref[...] = jnp.dot(hn_ref[...], w_ref[...], preferred_element_type=F32)

    def heads(fn):
        for h in range(N_HEADS):
            sl = slice(h * HEAD_DIM, (h + 1) * HEAD_DIM)
            main_ref[:, sl] = fn(acc_ref[:, sl]).astype(BF16)

    half_b = HEAD_DIM // ROPE_FRACTION // 2

    @pl.when(j == R_QA)
    def _():
        heads(lambda a: _rms(a, qkg_ref[0:1, :]))

    @pl.when(j == R_KA)
    def _():
        heads(lambda a: _rms(a, qkg_ref[1:2, :]))

    @pl.when(j == R_VA)
    def _():
        main_ref[...] = acc_ref[...].astype(BF16)

    @pl.when(j == R_QB)
    def _():
        heads(lambda a: _apply_rope(_rms(a, qkg_ref[2:3, :]), rope128_ref, half_b))

    @pl.when(j == R_KB)
    def _():
        heads(lambda a: _apply_rope(_rms(a, qkg_ref[3:4, :]), rope128_ref, half_b))

    @pl.when(j == R_QI)
    def _():
        heads(lambda a: _apply_rope(a, rope64_ref, IDX_DIM // ROPE_FRACTION // 2))


def _inproj(x2, pos2, g_mix, w_in, qk_a, qk_b):
    m = x2.shape[0]
    tm = TM_PROJ
    w = w_in.astype(BF16)
    o = [0, WIDTH, 2 * WIDTH, 3 * WIDTH, 4 * WIDTH, 5 * WIDTH, 6 * WIDTH, 7 * WIDTH]
    w_main = jnp.concatenate([w[:, o[0]:o[5]], w[:, o[6]:o[7]]], axis=1)
    w_vt = w[:, o[5]:o[6]].T
    w_ki = w[:, o[7]:o[7] + IDX_DIM]
    w_wi = w[:, o[7] + IDX_DIM:o[7] + IDX_DIM + IDX_HEADS]
    w_tail = jnp.concatenate([w_ki, w_ki, w_wi, jnp.zeros((D_MODEL, LANES - IDX_HEADS), BF16)], axis=1)
    qkg = jnp.concatenate([qk_a, qk_b, jnp.zeros((4, HEAD_DIM), F32)], axis=0).astype(F32)
    tab128 = jnp.asarray(_rope_lane_table(HEAD_DIM, HEAD_DIM // ROPE_FRACTION // 2))
    tab64 = jnp.asarray(_rope_lane_table(IDX_DIM, IDX_DIM // ROPE_FRACTION // 2))
    res = lambda shape: pl.BlockSpec(shape, lambda i, j: (0,) * len(shape))
    return pl.pallas_call(
        _inproj_kernel,
        grid=(m // tm, 6),
        in_specs=[
            pl.BlockSpec((tm, D_MODEL), lambda i, j: (i, 0)),
            res((1, D_MODEL)),
            pl.BlockSpec((D_MODEL, WIDTH), lambda i, j: (0, j)),
            res((WIDTH, D_MODEL)),
            res((D_MODEL, 2 * LANES)),
            res((SUBLANES, HEAD_DIM)),
            pl.BlockSpec((tm, 1), lambda i, j: (i, 0)),
            res((SUBLANES, LANES)),
            res((SUBLANES, LANES)),
        ],
        out_specs=[
            pl.BlockSpec((tm, WIDTH), lambda i, j: (i, j)),
            pl.BlockSpec((tm // TQ_DSA, WIDTH, TQ_DSA), lambda i, j: (i, 0, 0)),
            pl.BlockSpec((tm, 2 * LANES), lambda i, j: (i, 0)),
            pl.BlockSpec((tm, LANES), lambda i, j: (i, 0)),
        ],
        out_shape=[
            jax.ShapeDtypeStruct((m, 6 * WIDTH), BF16),
            jax.ShapeDtypeStruct((m // TQ_DSA, WIDTH, TQ_DSA), BF16),
            jax.ShapeDtypeStruct((m, 2 * LANES), BF16),
            jax.ShapeDtypeStruct((m, LANES), F32),
        ],
        scratch_shapes=[
            pltpu.VMEM((tm, D_MODEL), BF16),
            pltpu.VMEM((tm, WIDTH), F32),
            pltpu.VMEM((3, tm, LANES), F32),
            pltpu.VMEM((3, tm, LANES), F32),
        ],
        compiler_params=pltpu.CompilerParams(
            dimension_semantics=("parallel", "arbitrary"), vmem_limit_bytes=VMEM_LIMIT),
        name="inproj",
    )(x2, g_mix, w_main, w_vt, w_tail, qkg, pos2, tab128, tab64)


def _mixa_kernel(q_ref, k0_ref, k1_ref, k2_ref, v0_ref, v1_ref, v2_ref, bias_ref, o_ref):
    qb = pl.program_id(1)
    scale = HEAD_DIM ** -0.5
    k_refs = (k0_ref, k1_ref, k2_ref)
    v_refs = (v0_ref, v1_ref, v2_ref)
    pen = (jnp.where(qb >= 2, 0.0, NEG).astype(F32), jnp.where(qb >= 1, 0.0, NEG).astype(F32), None)
    for h in range(N_HEADS):
        sl = slice(h * HEAD_DIM, (h + 1) * HEAD_DIM)
        q = q_ref[:, sl]
        s = []
        for m in range(3):
            sm = _nt_dot(q, k_refs[m][:, sl]) * scale + bias_ref[h, :, m * QA_BLOCK:(m + 1) * QA_BLOCK]
            if pen[m] is not None:
                sm = sm + pen[m]
            s.append(sm)
        mx = jnp.maximum(jnp.maximum(jnp.max(s[0], axis=-1, keepdims=True), jnp.max(s[1], axis=-1, keepdims=True)),
                         jnp.max(s[2], axis=-1, keepdims=True))
        p = [jnp.exp(sm - mx) for sm in s]
        den = (jnp.sum(p[0], axis=-1, keepdims=True) + jnp.sum(p[1], axis=-1, keepdims=True)
               + jnp.sum(p[2], axis=-1, keepdims=True))
        o = jnp.dot(p[0].astype(BF16), v_refs[0][:, sl], preferred_element_type=F32)
        o = o + jnp.dot(p[1].astype(BF16), v_refs[1][:, sl], preferred_element_type=F32)
        o = o + jnp.dot(p[2].astype(BF16), v_refs[2][:, sl], preferred_element_type=F32)
        o_ref[:, sl] = (o / den).astype(BF16)


def _mixa(main, bias, batch, seq):
    m = main.shape[0]
    nqb = seq // QA_BLOCK

    def kv_spec(region, back):
        return pl.BlockSpec((QA_BLOCK, WIDTH),
                            lambda b, i: (b * nqb + jnp.maximum(i - back, 0), region))

    return pl.pallas_call(
        _mixa_kernel,
        grid=(batch, nqb),
        in_specs=[
            pl.BlockSpec((QA_BLOCK, WIDTH), lambda b, i: (b * nqb + i, R_QA)),
            kv_spec(R_KA, 2), kv_spec(R_KA, 1), kv_spec(R_KA, 0),
            kv_spec(R_VA, 2), kv_spec(R_VA, 1), kv_spec(R_VA, 0),
            pl.BlockSpec((N_HEADS, QA_BLOCK, QA_WIN), lambda b, i: (0, 0, 0)),
        ],
        out_specs=pl.BlockSpec((QA_BLOCK, WIDTH), lambda b, i: (b * nqb + i, 0)),
        out_shape=jax.ShapeDtypeStruct((m, WIDTH), BF16),
        compiler_params=pltpu.CompilerParams(
            dimension_semantics=("parallel", "parallel"), vmem_limit_bytes=VMEM_LIMIT),
        name="mixa",
    )(main, main, main, main, main, main, main, bias)


def _sortable(x):
    i = pltpu.bitcast(x, jnp.int32)
    return i ^ ((i >> 31) & jnp.int32(0x7FFFFFFF))


def _dsa_kernel(topk, qb_ref, kb_ref, vbt_ref, qi_ref, kiab_ref, wi_ref, o_ref, keys_ref, bias_ref):
    i = pl.program_id(1)
    tq = TQ_DSA
    nkt = i + 1
    scale = HEAD_DIM ** -0.5

    w_t = jnp.transpose(wi_ref[...]) * (IDX_HEADS ** -0.5 * IDX_DIM ** -0.5)
    qchunk = (i * tq + lax.broadcasted_iota(jnp.int32, (1, tq), 1)) // CHUNK
    row = lax.broadcasted_iota(jnp.int32, (tq, tq), 0)

    def admissible(kt):
        return (kt * tq + row) // CHUNK <= qchunk

    def idx_body(kt, carry):
        r0 = pl.multiple_of(kt * tq, tq)
        ka = kiab_ref[pl.ds(r0, tq), :LANES]
        kb = kiab_ref[pl.ds(r0, tq), LANES:]
        acc = jnp.zeros((tq, tq), F32)
        for p in range(IDX_HEADS // 2):
            qp = qi_ref[:, p * LANES:(p + 1) * LANES]
            acc = acc + w_t[2 * p:2 * p + 1, :] * jnp.maximum(_nt_dot(ka, qp), 0.0)
            acc = acc + w_t[2 * p + 1:2 * p + 2, :] * jnp.maximum(_nt_dot(kb, qp), 0.0)
        acc = jnp.where(admissible(kt), acc, -jnp.inf)
        keys_ref[pl.ds(r0, tq), :] = _sortable(acc)
        return carry

    lax.fori_loop(0, nkt, idx_body, 0)

    n_adm = (qchunk + 1) * CHUNK
    need = n_adm > topk

    def bit_body(t, thr):
        b = 31 - t
        cand = thr + lax.shift_left(jnp.int32(1), b)

        def cnt_body(kt, c):
            r0 = pl.multiple_of(kt * tq, tq)
            ge = (keys_ref[pl.ds(r0, tq), :] >= cand).astype(jnp.int32)
            return c + jnp.sum(ge, axis=0, keepdims=True)

        cnt = lax.fori_loop(0, nkt, cnt_body, jnp.zeros((1, tq), jnp.int32))
        return jnp.where(cnt >= topk, cand, thr)

    thr = lax.fori_loop(0, 32, bit_body, jnp.full((1, tq), jnp.iinfo(jnp.int32).min, jnp.int32))

    def mask_body(kt, carry):
        r0 = pl.multiple_of(kt * tq, tq)
        sel = admissible(kt) & ((keys_ref[pl.ds(r0, tq), :] >= thr) | jnp.logical_not(need))
        bias_ref[pl.ds(r0, tq), :] = jnp.where(sel, 0.0, NEG).astype(F32)
        return carry

    lax.fori_loop(0, nkt, mask_body, 0)

    for h in range(N_HEADS):
        sl = slice(h * HEAD_DIM, (h + 1) * HEAD_DIM)
        qh = qb_ref[:, sl]

        def att_body(kt, carry):
            m_run, l_run, acc = carry
            r0 = pl.multiple_of(kt * tq, tq)
            s = _nt_dot(kb_ref[pl.ds(r0, tq), sl], qh) * scale + bias_ref[pl.ds(r0, tq), :]
            m_new = jnp.maximum(m_run, jnp.max(s, axis=0, keepdims=True))
            alpha = jnp.exp(m_run - m_new)
            p = jnp.exp(s - m_new)
            l_new = alpha * l_run + jnp.sum(p, axis=0, keepdims=True)
            pv = jnp.dot(vbt_ref[kt, sl, :], p.astype(BF16), preferred_element_type=F32)
            return m_new, l_new, alpha * acc + pv

        init = (jnp.full((1, tq), NEG, F32), jnp.zeros((1, tq), F32), jnp.zeros((HEAD_DIM, tq), F32))
        _, l_fin, acc = lax.fori_loop(0, nkt, att_body, init)
        o_ref[:, sl] = jnp.transpose(acc / l_fin).astype(BF16)


def _dsa(main, vbt, kiab, wi, batch, seq):
    m = main.shape[0]
    tq = TQ_DSA
    nq = seq // tq
    topk = min(TOPK_MAX, seq // 4)
    return pl.pallas_call(
        functools.partial(_dsa_kernel, topk),
        grid=(batch, nq),
        in_specs=[
            pl.BlockSpec((tq, WIDTH), lambda b, i: (b * nq + i, R_QB)),
            pl.BlockSpec((seq, WIDTH), lambda b, i: (b, R_KB)),
            pl.BlockSpec((nq, WIDTH, tq), lambda b, i: (b, 0, 0)),
            pl.BlockSpec((tq, WIDTH), lambda b, i: (b * nq + i, R_QI)),
            pl.BlockSpec((seq, 2 * LANES), lambda b, i: (b, 0)),
            pl.BlockSpec((tq, LANES), lambda b, i: (b * nq + i, 0)),
        ],
        out_specs=pl.BlockSpec((tq, WIDTH), lambda b, i: (b * nq + i, 0)),
        out_shape=jax.ShapeDtypeStruct((m, WIDTH), BF16),
        scratch_shapes=[pltpu.VMEM((seq, tq), jnp.int32), pltpu.VMEM((seq, tq), F32)],
        compiler_params=pltpu.CompilerParams(
            dimension_semantics=("parallel", "arbitrary"), vmem_limit_bytes=VMEM_LIMIT),
        name="dsa",
    )(main, main, vbt, main, kiab, wi)


def _outproj_kernel(oa_ref, ob_ref, x_ref, wo_ref, g_ref, h1_ref, xn_ref):
    acc = jnp.dot(oa_ref[...], wo_ref[:WIDTH, :], preferred_element_type=F32)
    acc = acc + jnp.dot(ob_ref[...], wo_ref[WIDTH:, :], preferred_element_type=F32)
    h1 = x_ref[...] + acc
    h1_ref[...] = h1
    xn_ref[...] = _rms(h1, g_ref[...]).astype(BF16)


def _outproj(out_a, out_b, x2, w_out, g_ffn):
    m = x2.shape[0]
    tm = TM_PROJ
    return pl.pallas_call(
        _outproj_kernel,
        grid=(m // tm,),
        in_specs=[
            pl.BlockSpec((tm, WIDTH), lambda i: (i, 0)),
            pl.BlockSpec((tm, WIDTH), lambda i: (i, 0)),
            pl.BlockSpec((tm, D_MODEL), lambda i: (i, 0)),
            pl.BlockSpec((2 * WIDTH, D_MODEL), lambda i: (0, 0)),
            pl.BlockSpec((1, D_MODEL), lambda i: (0, 0)),
        ],
        out_specs=[pl.BlockSpec((tm, D_MODEL), lambda i: (i, 0)), pl.BlockSpec((tm, D_MODEL), lambda i: (i, 0))],
        out_shape=[jax.ShapeDtypeStruct((m, D_MODEL), F32), jax.ShapeDtypeStruct((m, D_MODEL), BF16)],
        compiler_params=pltpu.CompilerParams(dimension_semantics=("parallel",), vmem_limit_bytes=VMEM_LIMIT),
        name="outproj",
    )(out_a, out_b, x2, w_out.astype(BF16), g_ffn)


def _top_distinct(x):
    tm = x.shape[1]
    rows = lax.broadcasted_iota(jnp.int32, (PEER_TOPK, tm), 0)

    def body(r, carry):
        x, vals, cnts = carry
        mx = jnp.max(x, axis=0, keepdims=True)
        eq = x == mx
        n = jnp.sum(jnp.where(eq, 1.0, 0.0), axis=0, keepdims=True)
        hit = rows == r
        return jnp.where(eq, -jnp.inf, x), jnp.where(hit, mx, vals), jnp.where(hit, n, cnts)

    init = (x, jnp.full((PEER_TOPK, tm), -jnp.inf, F32), jnp.zeros((PEER_TOPK, tm), F32))
    _, vals, cnts = lax.fori_loop(0, PEER_TOPK, body, init)
    return vals, cnts


def _pair_threshold(v1, n1, v2, n2):
    tm = v1.shape[1]
    low = lax.broadcasted_iota(jnp.int32, (SUBLANES, tm), 0) < 4
    blocks = [(v1, n1, v2[0:1], n2[0:1], None)]
    for q in (1, 2, 3):
        blocks.append((v1[0:8], n1[0:8], v2[q:q + 1], n2[q:q + 1], None))
    blocks.append((v2[8:16], n2[8:16], v1[0:1], n1[0:1], None))
    for r in (0, 1, 2, 3):
        blocks.append((v2[0:8], n2[0:8], v1[r:r + 1], n1[r:r + 1], low))
    cand = jnp.concatenate(
        [a + b if mask is None else jnp.where(mask, -jnp.inf, a + b) for a, _, b, _, mask in blocks], axis=0)
    wgt = jnp.concatenate([na * nb for _, na, _, nb, _ in blocks], axis=0)

    def body(_, carry):
        cand, cum, tau = carry
        mx = jnp.max(cand, axis=0, keepdims=True)
        eq = cand == mx
        w = jnp.sum(jnp.where(eq, wgt, 0.0), axis=0, keepdims=True)
        tau = jnp.where(cum < PEER_TOPK, mx, tau)
        return jnp.where(eq, -jnp.inf, cand), cum + w, tau

    init = (cand, jnp.zeros((1, tm), F32), jnp.full((1, tm), jnp.inf, F32))
    _, _, tau = lax.fori_loop(0, PEER_TOPK, body, init)
    return tau


def _route_kernel(xn_ref, wq_ref, sk_ref, s2_ref, a2_ref, c_ref, a1_ref, qv_ref):
    qv_ref[...] = jnp.dot(xn_ref[...], wq_ref[...], preferred_element_type=F32).astype(BF16)
    for h in range(PEER_HEADS):
        c0 = 2 * h * PEER_KEYS
        s1 = _nt_dot(sk_ref[0], qv_ref[:, c0:c0 + PEER_KEYS])
        s2 = _nt_dot(sk_ref[1], qv_ref[:, c0 + PEER_KEYS:c0 + 2 * PEER_KEYS])
        v1, n1 = _top_distinct(s1)
        v2, n2 = _top_distinct(s2)
        tau = _pair_threshold(v1, n1, v2, n2)
        a1 = jnp.exp(s1 - v1[0:1])
        a2 = jnp.exp(s2 - v2[0:1])
        e2 = n2 * jnp.exp(v2 - v2[0:1])
        cth = jnp.full(s1.shape, jnp.inf, F32)
        mass = jnp.zeros(s1.shape, F32)
        pre = jnp.zeros_like(tau)
        for q in range(PEER_TOPK):
            v2q = v2[q:q + 1]
            pre = pre + e2[q:q + 1]
            ok = (s1 + v2q >= tau) & (v2q > -jnp.inf)
            cth = jnp.where(ok, v2q, cth)
            mass = jnp.where(ok, pre, mass)
        z = jnp.sum(a1 * mass, axis=0, keepdims=True)
        s2_ref[h] = s2
        a2_ref[h] = a2
        c_ref[h] = cth
        a1_ref[h] = a1 / z


def _route(xn, w_query, sub_keys):
    m = xn.shape[0]
    tm = TM_ROUTE
    spec = pl.BlockSpec((PEER_HEADS, PEER_KEYS, tm), lambda i: (0, 0, i))
    shp = jax.ShapeDtypeStruct((PEER_HEADS, PEER_KEYS, m), F32)
    return pl.pallas_call(
        _route_kernel,
        grid=(m // tm,),
        in_specs=[
            pl.BlockSpec((tm, D_MODEL), lambda i: (i, 0)),
            pl.BlockSpec((D_MODEL, 2 * PEER_HEADS * PEER_KEYS), lambda i: (0, 0)),
            pl.BlockSpec((2, PEER_KEYS, PEER_KEYS), lambda i: (0, 0, 0)),
        ],
        out_specs=[spec, spec, spec, spec],
        out_shape=[shp, shp, shp, shp],
        scratch_shapes=[pltpu.VMEM((tm, 2 * PEER_HEADS * PEER_KEYS), BF16)],
        compiler_params=pltpu.CompilerParams(dimension_semantics=("parallel",), vmem_limit_bytes=VMEM_LIMIT),
        name="route",
    )(xn, w_query.astype(BF16), sub_keys.astype(BF16))


def _experts_kernel(xn_ref, u_ref, vt_ref, s2_ref, a2_ref, c_ref, a1_ref, o_ref, coef_ref):
    j = pl.program_id(1)
    n_e1 = TE_EXP // PEER_KEYS

    @pl.when(j == 0)
    def _():
        o_ref[...] = jnp.zeros_like(o_ref)

    act = jax.nn.gelu(_nt_dot(u_ref[...], xn_ref[...]))
    for e in range(n_e1):
        e1 = j * n_e1 + e
        gate = jnp.zeros((PEER_KEYS, xn_ref.shape[0]), F32)
        for h in range(PEER_HEADS):
            cth = c_ref[h, pl.ds(e1, 1), :]
            a1 = a1_ref[h, pl.ds(e1, 1), :]
            gate = gate + jnp.where(s2_ref[h] >= cth, a2_ref[h], 0.0) * a1
        coef_ref[e * PEER_KEYS:(e + 1) * PEER_KEYS, :] = (
            gate * act[e * PEER_KEYS:(e + 1) * PEER_KEYS, :]).astype(BF16)
    o_ref[...] += jnp.dot(vt_ref[...], coef_ref[...], preferred_element_type=F32)


def _experts(xn, expert_u, expert_v, s2, a2, cth, a1):
    m = xn.shape[0]
    tm, te = TM_EXP, TE_EXP
    n_exp = expert_u.shape[0]
    gspec = pl.BlockSpec((PEER_HEADS, PEER_KEYS, tm), lambda i, j: (0, 0, i))
    return pl.pallas_call(
        _experts_kernel,
        grid=(m // tm, n_exp // te),
        in_specs=[
            pl.BlockSpec((tm, D_MODEL), lambda i, j: (i, 0)),
            pl.BlockSpec((te, D_MODEL), lambda i, j: (j, 0)),
            pl.BlockSpec((D_MODEL, te), lambda i, j: (0, j)),
            gspec, gspec, gspec, gspec,
        ],
        out_specs=pl.BlockSpec((D_MODEL, tm), lambda i, j: (0, i)),
        out_shape=jax.ShapeDtypeStruct((D_MODEL, m), F32),
        scratch_shapes=[pltpu.VMEM((te, tm), BF16)],
        compiler_params=pltpu.CompilerParams(
            dimension_semantics=("parallel", "arbitrary"), vmem_limit_bytes=VMEM_LIMIT),
        name="experts",
    )(xn, expert_u.astype(BF16), expert_v.astype(BF16).T, s2, a2, cth, a1)


def _final_kernel(h1_ref, ot_ref, p_ref, g_ref, wg_ref, wp_ref, o_ref):
    h2 = h1_ref[...] + jnp.transpose(ot_ref[...])
    hn = _rms(h2, g_ref[...]).astype(BF16)
    gate = jax.nn.sigmoid(jnp.dot(hn, wg_ref[...], preferred_element_type=F32))
    pe = jnp.dot(p_ref[...].astype(BF16), wp_ref[...], preferred_element_type=F32)
    o_ref[...] = h2 + gate * pe


def _final(h1, out_t, p2, g_ple, w_gate, w_proj):
    m = h1.shape[0]
    tm = TM_FINAL
    return pl.pallas_call(
        _final_kernel,
        grid=(m // tm,),
        in_specs=[
            pl.BlockSpec((tm, D_MODEL), lambda i: (i, 0)),
            pl.BlockSpec((D_MODEL, tm), lambda i: (0, i)),
            pl.BlockSpec((tm, PLE_DIM), lambda i: (i, 0)),
            pl.BlockSpec((1, D_MODEL), lambda i: (0, 0)),
            pl.BlockSpec((D_MODEL, D_MODEL), lambda i: (0, 0)),
            pl.BlockSpec((PLE_DIM, D_MODEL), lambda i: (0, 0)),
        ],
        out_specs=pl.BlockSpec((tm, D_MODEL), lambda i: (i, 0)),
        out_shape=jax.ShapeDtypeStruct((m, D_MODEL), F32),
        compiler_params=pltpu.CompilerParams(dimension_semantics=("parallel",), vmem_limit_bytes=VMEM_LIMIT),
        name="final",
    )(h1, out_t, p2, g_ple, w_gate.astype(BF16), w_proj.astype(BF16))


def _layer(h2d, p2d, pos2, batch, seq, norm_mix, w_in, qk_a, qk_b, rel_bias, w_out, norm_ffn,
           peer_query, sub_keys, peer_u, peer_v, norm_ple, ple_gate, ple_proj):
    bias = _relbias_block(rel_bias)
    main, vbt, kiab, wi = _inproj(h2d, pos2, norm_mix[None, :], w_in, qk_a, qk_b)
    out_a = _mixa(main, bias, batch, seq)
    out_b = _dsa(main, vbt, kiab, wi, batch, seq)
    h1, xn = _outproj(out_a, out_b, h2d, w_out, norm_ffn[None, :])
    s2, a2, cth, a1 = _route(xn, peer_query, sub_keys)
    out_t = _experts(xn, peer_u, peer_v, s2, a2, cth, a1)
    return _final(h1, out_t, p2d, norm_ple[None, :], ple_gate, ple_proj)


def kernel(x, p, positions, norm_mix, w_in, qk_norm_a, qk_norm_b, rel_bias, w_out, norm_ffn, peer_query,
           peer_sub_keys, peer_u, peer_v, norm_ple, ple_gate, ple_proj):
    batch, seq, d = x.shape
    assert d == D_MODEL and seq % TM_PROJ == 0 and (batch * seq) % TM_PROJ == 0
    h = x.reshape(batch * seq, d)
    pos2 = positions.reshape(batch * seq, 1).astype(jnp.int32)
    for i in range(p.shape[0]):
        h = _layer(h, p[i].reshape(batch * seq, PLE_DIM), pos2, batch, seq, norm_mix[i], w_in[i],
                   qk_norm_a[i], qk_norm_b[i], rel_bias[i], w_out[i], norm_ffn[i], peer_query[i],
                   peer_sub_keys[i], peer_u[i], peer_v[i], norm_ple[i], ple_gate[i], ple_proj[i])
    return h.reshape(batch, seq, d)
```

```python
import functools

import numpy as np
import jax
import jax.numpy as jnp
from jax import lax
from jax.experimental import pallas as pl
from jax.experimental.pallas import tpu as pltpu

F32 = jnp.float32
BF16 = jnp.bfloat16

D_MODEL = 2048
HEAD_DIM = 128
N_HEADS = 8
WIDTH = N_HEADS * HEAD_DIM
CHUNK = 64
LEFT_CHUNKS = 8
MAX_REL = 128
IDX_HEADS = 16
IDX_DIM = 64
TOPK_MAX = 256
ROPE_THETA = 500000.0
ROPE_FRACTION = 4
PEER_HEADS = 8
PEER_KEYS = 128
PEER_TOPK = 16
PLE_DIM = 256
EPS = 1e-6

LANES = 128
SUBLANES = 8
VMEM_LIMIT = 56 * 1024 * 1024

NEG = -1e30

TM_PROJ = 512
QA_BLOCK = 256
QA_WIN = 768
TQ_DSA = 256
KSUB_DSA = 128
BITS_PER_CHECK = 4
TM_ROUTE = 512
TM_EXP = 512
TE_EXP = 512
TM_FINAL = 256
GATE_ROWS = 64
TP_EXP = 256

_NT = (((1,), (1,)), ((), ()))


def _nt_dot(a, b):
    return lax.dot_general(a, b, _NT, preferred_element_type=F32)


def _rms(x, g):
    return x * lax.rsqrt(jnp.mean(x * x, axis=-1, keepdims=True) + EPS) * g


def _rope_lane_table(group, half):
    lane = np.arange(LANES)
    m = lane % group
    inv = (np.float32(1.0) / np.power(np.float32(ROPE_THETA),
                                      (np.arange(half, dtype=np.float32) / np.float32(half)))).astype(np.float32)
    tab = np.zeros((SUBLANES, LANES), np.float32)
    tab[0] = np.where(m < 2 * half, inv[m % half], 0.0)
    tab[1] = (m < half)
    tab[2] = (m >= half) & (m < 2 * half)
    return tab


def _rope_coeffs(pos_f, tab_ref, out_ref):
    ang = pos_f * tab_ref[0:1, :]
    s = jnp.sin(ang)
    out_ref[0] = jnp.cos(ang)
    out_ref[1] = -s * tab_ref[1:2, :]
    out_ref[2] = s * tab_ref[2:3, :]


def _apply_rope(x, coef_ref, half):
    up = pltpu.roll(x, LANES - half, 1)
    dn = pltpu.roll(x, half, 1)
    return x * coef_ref[0] + up * coef_ref[1] + dn * coef_ref[2]


def _relbias_kernel(u_ref, o_ref):
    x = jnp.broadcast_to(u_ref[0, 0:1, :], (QA_BLOCK, 1024))
    y = pltpu.roll(x, 0, 1, stride=1, stride_axis=0)[:, :QA_WIN]
    qc = lax.broadcasted_iota(jnp.int32, (QA_BLOCK, QA_WIN), 0) // CHUNK
    kc = lax.broadcasted_iota(jnp.int32, (QA_BLOCK, QA_WIN), 1) // CHUNK
    band = (kc >= qc) & (kc <= qc + LEFT_CHUNKS)
    o_ref[0] = jnp.where(band, y, NEG)


def _relbias_block(rel_bias):
    p = np.arange(1024)
    off = np.where(p < QA_WIN, p, p - 1024)
    idx = np.clip(LEFT_CHUNKS * CHUNK - off, -MAX_REL, MAX_REL) + MAX_REL
    u = jnp.take(rel_bias.astype(F32), jnp.asarray(idx, jnp.int32), axis=1)
    u = jnp.broadcast_to(u[:, None, :], (N_HEADS, SUBLANES, 1024))
    return pl.pallas_call(
        _relbias_kernel,
        grid=(N_HEADS,),
        in_specs=[pl.BlockSpec((1, SUBLANES, 1024), lambda h: (h, 0, 0))],
        out_specs=pl.BlockSpec((1, QA_BLOCK, QA_WIN), lambda h: (h, 0, 0)),
        out_shape=jax.ShapeDtypeStruct((N_HEADS, QA_BLOCK, QA_WIN), F32),
        name="relbias",
    )(u)


R_QA, R_KA, R_VA, R_QB, R_KB, R_QI = range(6)


def _inproj_kernel(x_ref, g_ref, w_ref, wvt_ref, wtail_ref, qkg_ref, pos_ref, tab128_ref, tab64_ref,
                   main_ref, vbt_ref, kiab_ref, wi_ref,
                   hn_ref, acc_ref, rope128_ref, rope64_ref):
    j = pl.program_id(1)
    tm = x_ref.shape[0]

    @pl.when(j == 0)
    def _():
        hn_ref[...] = _rms(x_ref[...], g_ref[...]).astype(BF16)
        pos_f = pos_ref[...].astype(F32)
        _rope_coeffs(pos_f, tab128_ref, rope128_ref)
        _rope_coeffs(pos_f, tab64_ref, rope64_ref)
        vt = _nt_dot(wvt_ref[...], hn_ref[...]).astype(BF16)
        for t in range(tm // TQ_DSA):
            vbt_ref[t] = vt[:, t * TQ_DSA:(t + 1) * TQ_DSA]
        tail = jnp.dot(hn_ref[...], wtail_ref[...], preferred_element_type=F32)
        kk = _apply_rope(tail[:, :LANES], rope64_ref, IDX_DIM // ROPE_FRACTION // 2)
        lane = lax.broadcasted_iota(jnp.int32, (tm, LANES), 1)
        kiab_ref[:, :LANES] = jnp.where(lane < IDX_DIM, kk, 0.0).astype(BF16)
        kiab_ref[:, LANES:] = jnp.where(lane >= IDX_DIM, kk, 0.0).astype(BF16)
        wi_ref[...] = tail[:, LANES:]

    acc_ref[...] = jnp.dot(hn_ref[...], w_ref[...], preferred_element_type=F32)

    def heads(fn):
        for h in range(N_HEADS):
            sl = slice(h * HEAD_DIM, (h + 1) * HEAD_DIM)
            main_ref[:, sl] = fn(acc_ref[:, sl]).astype(BF16)

    half_b = HEAD_DIM // ROPE_FRACTION // 2

    @pl.when(j == R_QA)
    def _():
        heads(lambda a: _rms(a, qkg_ref[0:1, :]))

    @pl.when(j == R_KA)
    def _():
        heads(lambda a: _rms(a, qkg_ref[1:2, :]))

    @pl.when(j == R_VA)
    def _():
        main_ref[...] = acc_ref[...].astype(BF16)

    @pl.when(j == R_QB)
    def _():
        heads(lambda a: _apply_rope(_rms(a, qkg_ref[2:3, :]), rope128_ref, half_b))

    @pl.when(j == R_KB)
    def _():
        heads(lambda a: _apply_rope(_rms(a, qkg_ref[3:4, :]), rope128_ref, half_b))

    @pl.when(j == R_QI)
    def _():
        heads(lambda a: _apply_rope(a, rope64_ref, IDX_DIM // ROPE_FRACTION // 2))


def _inproj(x2, pos2, g_mix, w_in, qk_a, qk_b):
    m = x2.shape[0]
    tm = TM_PROJ
    w = w_in.astype(BF16)
    o = [0, WIDTH, 2 * WIDTH, 3 * WIDTH, 4 * WIDTH, 5 * WIDTH, 6 * WIDTH, 7 * WIDTH]
    w_main = jnp.concatenate([w[:, o[0]:o[5]], w[:, o[6]:o[7]]], axis=1)
    w_vt = w[:, o[5]:o[6]].T
    w_ki = w[:, o[7]:o[7] + IDX_DIM]
    w_wi = w[:, o[7] + IDX_DIM:o[7] + IDX_DIM + IDX_HEADS]
    w_tail = jnp.concatenate([w_ki, w_ki, w_wi, jnp.zeros((D_MODEL, LANES - IDX_HEADS), BF16)], axis=1)
    qkg = jnp.concatenate([qk_a, qk_b, jnp.zeros((4, HEAD_DIM), F32)], axis=0).astype(F32)
    tab128 = jnp.asarray(_rope_lane_table(HEAD_DIM, HEAD_DIM // ROPE_FRACTION // 2))
    tab64 = jnp.asarray(_rope_lane_table(IDX_DIM, IDX_DIM // ROPE_FRACTION // 2))
    res = lambda shape: pl.BlockSpec(shape, lambda i, j: (0,) * len(shape))
    return pl.pallas_call(
        _inproj_kernel,
        grid=(m // tm, 6),
        in_specs=[
            pl.BlockSpec((tm, D_MODEL), lambda i, j: (i, 0)),
            res((1, D_MODEL)),
            pl.BlockSpec((D_MODEL, WIDTH), lambda i, j: (0, j)),
            res((WIDTH, D_MODEL)),
            res((D_MODEL, 2 * LANES)),
            res((SUBLANES, HEAD_DIM)),
            pl.BlockSpec((tm, 1), lambda i, j: (i, 0)),
            res((SUBLANES, LANES)),
            res((SUBLANES, LANES)),
        ],
        out_specs=[
            pl.BlockSpec((tm, WIDTH), lambda i, j: (i, j)),
            pl.BlockSpec((tm // TQ_DSA, WIDTH, TQ_DSA), lambda i, j: (i, 0, 0)),
            pl.BlockSpec((tm, 2 * LANES), lambda i, j: (i, 0)),
            pl.BlockSpec((tm, LANES), lambda i, j: (i, 0)),
        ],
        out_shape=[
            jax.ShapeDtypeStruct((m, 6 * WIDTH), BF16),
            jax.ShapeDtypeStruct((m // TQ_DSA, WIDTH, TQ_DSA), BF16),
            jax.ShapeDtypeStruct((m, 2 * LANES), BF16),
            jax.ShapeDtypeStruct((m, LANES), F32),
        ],
        scratch_shapes=[
            pltpu.VMEM((tm, D_MODEL), BF16),
            pltpu.VMEM((tm, WIDTH), F32),
            pltpu.VMEM((3, tm, LANES), F32),
            pltpu.VMEM((3, tm, LANES), F32),
        ],
        compiler_params=pltpu.CompilerParams(
            dimension_semantics=("parallel", "arbitrary"), vmem_limit_bytes=VMEM_LIMIT),
        name="inproj",
    )(x2, g_mix, w_main, w_vt, w_tail, qkg, pos2, tab128, tab64)


def _mixa_kernel(q_ref, k0_ref, k1_ref, k2_ref, v0_ref, v1_ref, v2_ref, bias_ref, o_ref):
    qb = pl.program_id(1)
    scale = HEAD_DIM ** -0.5
    k_refs = (k0_ref, k1_ref, k2_ref)
    v_refs = (v0_ref, v1_ref, v2_ref)
    pen = (jnp.where(qb >= 2, 0.0, NEG).astype(F32), jnp.where(qb >= 1, 0.0, NEG).astype(F32), None)
    for h in range(N_HEADS):
        sl = slice(h * HEAD_DIM, (h + 1) * HEAD_DIM)
        q = q_ref[:, sl]
        s = []
        for m in range(3):
            sm = _nt_dot(q, k_refs[m][:, sl]) * scale + bias_ref[h, :, m * QA_BLOCK:(m + 1) * QA_BLOCK]
            if pen[m] is not None:
                sm = sm + pen[m]
            s.append(sm)
        mx = jnp.maximum(jnp.maximum(jnp.max(s[0], axis=-1, keepdims=True), jnp.max(s[1], axis=-1, keepdims=True)),
                         jnp.max(s[2], axis=-1, keepdims=True))
        p = [jnp.exp(sm - mx) for sm in s]
        den = (jnp.sum(p[0], axis=-1, keepdims=True) + jnp.sum(p[1], axis=-1, keepdims=True)
               + jnp.sum(p[2], axis=-1, keepdims=True))
        o = jnp.dot(p[0].astype(BF16), v_refs[0][:, sl], preferred_element_type=F32)
        o = o + jnp.dot(p[1].astype(BF16), v_refs[1][:, sl], preferred_element_type=F32)
        o = o + jnp.dot(p[2].astype(BF16), v_refs[2][:, sl], preferred_element_type=F32)
        o_ref[:, sl] = (o / den).astype(BF16)


def _mixa(main, bias, batch, seq):
    m = main.shape[0]
    nqb = seq // QA_BLOCK

    def kv_spec(region, back):
        return pl.BlockSpec((QA_BLOCK, WIDTH),
                            lambda b, i: (b * nqb + jnp.maximum(i - back, 0), region))

    return pl.pallas_call(
        _mixa_kernel,
        grid=(batch, nqb),
        in_specs=[
            pl.BlockSpec((QA_BLOCK, WIDTH), lambda b, i: (b * nqb + i, R_QA)),
            kv_spec(R_KA, 2), kv_spec(R_KA, 1), kv_spec(R_KA, 0),
            kv_spec(R_VA, 2), kv_spec(R_VA, 1), kv_spec(R_VA, 0),
            pl.BlockSpec((N_HEADS, QA_BLOCK, QA_WIN), lambda b, i: (0, 0, 0)),
        ],
        out_specs=pl.BlockSpec((QA_BLOCK, WIDTH), lambda b, i: (b * nqb + i, 0)),
        out_shape=jax.ShapeDtypeStruct((m, WIDTH), BF16),
        compiler_params=pltpu.CompilerParams(
            dimension_semantics=("parallel", "parallel"), vmem_limit_bytes=VMEM_LIMIT),
        name="mixa",
    )(main, main, main, main, main, main, main, bias)


def _key_to_f32(t):
    return pltpu.bitcast(t ^ ((t >> 31) & jnp.int32(0x7FFFFFFF)), F32)


def _dsa_kernel(topk, qb_ref, kb_ref, vbt_ref, qi_ref, kiab_ref, wi_ref, o_ref,
                sc_ref, bias_ref, acc_ref, m_ref, l_ref):
    i = pl.program_id(1)
    tq = TQ_DSA
    nkt = i + 1
    c2 = HEAD_DIM ** -0.5 * 1.4426950408889634

    w_t = jnp.transpose(wi_ref[...]) * (IDX_HEADS ** -0.5 * IDX_DIM ** -0.5)
    qchunk = (i * tq + lax.broadcasted_iota(jnp.int32, (1, tq), 1)) // CHUNK
    row = lax.broadcasted_iota(jnp.int32, (tq, tq), 0)

    def admissible(kt):
        return (kt * tq + row) // CHUNK <= qchunk

    def idx_body(kt, carry):
        r0 = pl.multiple_of(kt * tq, tq)
        ka = kiab_ref[pl.ds(r0, tq), :LANES]
        kb = kiab_ref[pl.ds(r0, tq), LANES:]
        acc = jnp.zeros((tq, tq), F32)
        for p in range(IDX_HEADS // 2):
            qp = qi_ref[:, p * LANES:(p + 1) * LANES]
            acc = acc + w_t[2 * p:2 * p + 1, :] * jnp.maximum(_nt_dot(ka, qp), 0.0)
            acc = acc + w_t[2 * p + 1:2 * p + 2, :] * jnp.maximum(_nt_dot(kb, qp), 0.0)
        sc_ref[pl.ds(r0, tq), :] = jnp.where(admissible(kt), acc, -jnp.inf)
        return carry

    lax.fori_loop(0, nkt, idx_body, 0)

    n_adm = (qchunk + 1) * CHUNK
    need = n_adm > topk

    def count_ge(c_f):
        def cnt_body(kt, c):
            r0 = pl.multiple_of(kt * tq, tq)
            ge = jnp.where(sc_ref[pl.ds(r0, tq), :] >= c_f, 1, 0).astype(jnp.int32)
            return c + jnp.sum(ge.reshape(tq // SUBLANES, SUBLANES, tq), axis=0)

        c8 = lax.fori_loop(0, nkt, cnt_body, jnp.zeros((SUBLANES, tq), jnp.int32))
        return jnp.sum(c8, axis=0, keepdims=True)

    def bit_cond(carry):
        b, _, cnt = carry
        todo = jnp.max(jnp.where(need & (cnt != topk), 1, 0))
        return (b >= 0) & (todo > 0)

    def bit_body(carry):
        b, thr, cnt = carry
        for _ in range(BITS_PER_CHECK):
            cand = thr + lax.shift_left(jnp.int32(1), b)
            n = count_ge(_key_to_f32(cand))
            ok = n >= topk
            b, thr, cnt = b - 1, jnp.where(ok, cand, thr), jnp.where(ok, n, cnt)
        return b, thr, cnt

    init = (jnp.int32(31), jnp.full((1, tq), jnp.iinfo(jnp.int32).min, jnp.int32),
            jnp.full((1, tq), jnp.iinfo(jnp.int32).max, jnp.int32))
    _, thr, _ = lax.while_loop(bit_cond, bit_body, init)
    thr_f = _key_to_f32(thr)

    def mask_body(kt, carry):
        r0 = pl.multiple_of(kt * tq, tq)
        sel = admissible(kt) & ((sc_ref[pl.ds(r0, tq), :] >= thr_f) | jnp.logical_not(need))
        bias_ref[pl.ds(r0, tq), :] = jnp.where(sel, 0.0, NEG).astype(F32)
        return carry

    lax.fori_loop(0, nkt, mask_body, 0)

    acc_ref[...] = jnp.zeros_like(acc_ref)
    m_ref[...] = jnp.full(m_ref.shape, NEG, F32)
    l_ref[...] = jnp.zeros_like(l_ref)

    def att_body(kt, carry):
        for half in range(tq // KSUB_DSA):
            r0 = pl.multiple_of(kt * tq + half * KSUB_DSA, KSUB_DSA)
            ks = slice(half * KSUB_DSA, (half + 1) * KSUB_DSA)
            for h in range(N_HEADS):
                sl = slice(h * HEAD_DIM, (h + 1) * HEAD_DIM)
                s = _nt_dot(kb_ref[pl.ds(r0, KSUB_DSA), sl], qb_ref[:, sl]) + bias_ref[pl.ds(r0, KSUB_DSA), :]
                m_old = m_ref[h:h + 1, :]
                m_new = jnp.maximum(m_old, jnp.max(s, axis=0, keepdims=True))
                alpha = jnp.exp2((m_old - m_new) * c2)
                p = jnp.exp2((s - m_new) * c2)
                l_ref[h:h + 1, :] = alpha * l_ref[h:h + 1, :] + jnp.sum(p, axis=0, keepdims=True)
                m_ref[h:h + 1, :] = m_new
                pv = jnp.dot(vbt_ref[kt, sl, ks], p.astype(BF16), preferred_element_type=F32)
                acc_ref[h] = alpha * acc_ref[h] + pv
        return carry

    lax.fori_loop(0, nkt, att_body, 0)
    for h in range(N_HEADS):
        sl = slice(h * HEAD_DIM, (h + 1) * HEAD_DIM)
        o_ref[:, sl] = jnp.transpose(acc_ref[h] / l_ref[h:h + 1, :]).astype(BF16)


def _dsa(main, vbt, kiab, wi, batch, seq):
    m = main.shape[0]
    tq = TQ_DSA
    nq = seq // tq
    topk = min(TOPK_MAX, seq // 4)
    return pl.pallas_call(
        functools.partial(_dsa_kernel, topk),
        grid=(batch, nq),
        in_specs=[
            pl.BlockSpec((tq, WIDTH), lambda b, i: (b * nq + i, R_QB)),
            pl.BlockSpec((seq, WIDTH), lambda b, i: (b, R_KB)),
            pl.BlockSpec((nq, WIDTH, tq), lambda b, i: (b, 0, 0)),
            pl.BlockSpec((tq, WIDTH), lambda b, i: (b * nq + i, R_QI)),
            pl.BlockSpec((seq, 2 * LANES), lambda b, i: (b, 0)),
            pl.BlockSpec((tq, LANES), lambda b, i: (b * nq + i, 0)),
        ],
        out_specs=pl.BlockSpec((tq, WIDTH), lambda b, i: (b * nq + i, 0)),
        out_shape=jax.ShapeDtypeStruct((m, WIDTH), BF16),
        scratch_shapes=[pltpu.VMEM((seq, tq), F32), pltpu.VMEM((seq, tq), F32),
                        pltpu.VMEM((N_HEADS, HEAD_DIM, tq), F32),
                        pltpu.VMEM((N_HEADS, tq), F32), pltpu.VMEM((N_HEADS, tq), F32)],
        compiler_params=pltpu.CompilerParams(
            dimension_semantics=("parallel", "arbitrary"), vmem_limit_bytes=VMEM_LIMIT),
        name="dsa",
    )(main, main, vbt, main, kiab, wi)


def _outproj_kernel(oa_ref, ob_ref, x_ref, wo_ref, g_ref, h1_ref, xn_ref):
    acc = jnp.dot(oa_ref[...], wo_ref[:WIDTH, :], preferred_element_type=F32)
    acc = acc + jnp.dot(ob_ref[...], wo_ref[WIDTH:, :], preferred_element_type=F32)
    h1 = x_ref[...] + acc
    h1_ref[...] = h1
    xn_ref[...] = _rms(h1, g_ref[...]).astype(BF16)


def _outproj(out_a, out_b, x2, w_out, g_ffn):
    m = x2.shape[0]
    tm = TM_PROJ
    return pl.pallas_call(
        _outproj_kernel,
        grid=(m // tm,),
        in_specs=[
            pl.BlockSpec((tm, WIDTH), lambda i: (i, 0)),
            pl.BlockSpec((tm, WIDTH), lambda i: (i, 0)),
            pl.BlockSpec((tm, D_MODEL), lambda i: (i, 0)),
            pl.BlockSpec((2 * WIDTH, D_MODEL), lambda i: (0, 0)),
            pl.BlockSpec((1, D_MODEL), lambda i: (0, 0)),
        ],
        out_specs=[pl.BlockSpec((tm, D_MODEL), lambda i: (i, 0)), pl.BlockSpec((tm, D_MODEL), lambda i: (i, 0))],
        out_shape=[jax.ShapeDtypeStruct((m, D_MODEL), F32), jax.ShapeDtypeStruct((m, D_MODEL), BF16)],
        compiler_params=pltpu.CompilerParams(dimension_semantics=("parallel",), vmem_limit_bytes=VMEM_LIMIT),
        name="outproj",
    )(out_a, out_b, x2, w_out.astype(BF16), g_ffn)


def _merge_exchange_pairs(n):
    t = (n - 1).bit_length()
    pairs = []
    p = 1 << (t - 1)
    while p > 0:
        q, r, d = 1 << (t - 1), 0, p
        while True:
            pairs.extend((i, i + d) for i in range(n - d) if (i & p) == r)
            if q == p:
                break
            d, q, r = q - p, q >> 1, p
        p >>= 1
    return pairs


_SORT16 = _merge_exchange_pairs(PEER_TOPK)


def _tmax(a, b):
    if a is None:
        return b
    if b is None:
        return a
    return jnp.maximum(a, b)


def _cmp_exchange(x, i, j):
    a, b = x[i], x[j]
    if b is None:
        return
    if a is None:
        x[i], x[j] = b, None
        return
    x[i], x[j] = jnp.maximum(a, b), jnp.minimum(a, b)


def _sort16_desc(x):
    for i, j in _SORT16:
        _cmp_exchange(x, i, j)


def _bitonic16_desc(x):
    for d in (8, 4, 2, 1):
        for i in range(PEER_TOPK):
            if not i & d:
                _cmp_exchange(x, i, i + d)


def _merge_sublanes(x, shift, sort=True):
    y = [None if v is None else pltpu.roll(v, shift, 0) for v in x]
    out = [_tmax(x[i], y[PEER_TOPK - 1 - i]) for i in range(PEER_TOPK)]
    if sort:
        _bitonic16_desc(out)
    return out


def _top16(tiles):
    x = list(tiles)
    _sort16_desc(x)
    for shift in (4, 2, 1):
        x = _merge_sublanes(x, shift)
    return x


def _pair_threshold(v1, v2):
    sub = lax.broadcasted_iota(jnp.int32, v1[0].shape, 0)

    def diag(vals):
        d = vals[0]
        for s in range(1, SUBLANES):
            d = jnp.where(sub == s, vals[s], d)
        return d

    d1a, d1b = diag(v1[:8]), diag(v1[8:])
    d2a, d2b = diag(v2[:8]), diag(v2[8:])
    cand = [d1a + v2[0], d1b + v2[0], d1a + v2[1], d1a + v2[2], d1a + v2[3], v1[0] + d2b]
    cand += [jnp.where(sub < 4, -jnp.inf, v1[r] + d2a) for r in range(3)]
    x = cand + [None] * (PEER_TOPK - len(cand))
    _sort16_desc(x)
    x = _merge_sublanes(x, 4)
    x = _merge_sublanes(x, 2)
    x = _merge_sublanes(x, 1, sort=False)
    tau = x[0]
    for v in x[1:]:
        tau = jnp.minimum(tau, v)
    return tau


def _sublane_total(x):
    for shift in (4, 2, 1):
        x = x + pltpu.roll(x, shift, 0)
    return x


def _route_kernel(xn_ref, wq_ref, sk_ref, s2_ref, a2_ref, c_ref, a1_ref, qv_ref):
    tm = xn_ref.shape[0]
    n_t = PEER_KEYS // SUBLANES
    qv = jnp.dot(xn_ref[...], wq_ref[...], preferred_element_type=F32)
    for c in range(2 * PEER_HEADS):
        qv_ref[c] = qv[:, c * PEER_KEYS:(c + 1) * PEER_KEYS].astype(BF16)

    def chunk(h, lc):
        t0 = pl.multiple_of(lc * LANES, LANES)
        s1 = _nt_dot(sk_ref[0], qv_ref[2 * h, pl.ds(t0, LANES), :])
        s2 = _nt_dot(sk_ref[1], qv_ref[2 * h + 1, pl.ds(t0, LANES), :])
        s1 = [s1[j * SUBLANES:(j + 1) * SUBLANES, :] for j in range(n_t)]
        s2 = [s2[j * SUBLANES:(j + 1) * SUBLANES, :] for j in range(n_t)]
        v1 = _top16(s1)
        v2 = _top16(s2)
        tau = _pair_threshold(v1, v2)
        pre = []
        for q in range(PEER_TOPK):
            e = jnp.exp(v2[q] - v2[0])
            pre.append(e if q == 0 else pre[-1] + e)
        a1, cth, z = [], [], None
        for j in range(n_t):
            c_j = jnp.full(s1[j].shape, jnp.inf, F32)
            m_j = jnp.zeros(s1[j].shape, F32)
            for q in range(PEER_TOPK):
                ok = s1[j] + v2[q] >= tau
                c_j = jnp.where(ok, v2[q], c_j)
                m_j = jnp.where(ok, pre[q], m_j)
            a_j = jnp.exp(s1[j] - v1[0])
            a1.append(a_j)
            cth.append(c_j)
            z = a_j * m_j if z is None else z + a_j * m_j
        z = _sublane_total(z)
        for j in range(n_t):
            rs = slice(j * SUBLANES, (j + 1) * SUBLANES)
            s2_ref[h, lc, rs, :] = s2[j]
            a2_ref[h, lc, rs, :] = jnp.exp(s2[j] - v2[0])
            c_ref[h, lc, rs, :] = cth[j]
            a1_ref[h, lc, rs, :] = a1[j] / z

    def head_body(h, carry):
        def chunk_body(lc, c):
            chunk(h, lc)
            return c

        return lax.fori_loop(0, tm // LANES, chunk_body, carry)

    lax.fori_loop(0, PEER_HEADS, head_body, 0)


def _route(xn, w_query, sub_keys):
    m = xn.shape[0]
    tm = TM_ROUTE
    spec = pl.BlockSpec((PEER_HEADS, tm // LANES, PEER_KEYS, LANES), lambda i: (0, i, 0, 0))
    shp = jax.ShapeDtypeStruct((PEER_HEADS, m // LANES, PEER_KEYS, LANES), F32)
    return pl.pallas_call(
        _route_kernel,
        grid=(m // tm,),
        in_specs=[
            pl.BlockSpec((tm, D_MODEL), lambda i: (i, 0)),
            pl.BlockSpec((D_MODEL, 2 * PEER_HEADS * PEER_KEYS), lambda i: (0, 0)),
            pl.BlockSpec((2, PEER_KEYS, PEER_KEYS), lambda i: (0, 0, 0)),
        ],
        out_specs=[spec, spec, spec, spec],
        out_shape=[shp, shp, shp, shp],
        scratch_shapes=[pltpu.VMEM((2 * PEER_HEADS, tm, PEER_KEYS), BF16)],
        compiler_params=pltpu.CompilerParams(dimension_semantics=("parallel",), vmem_limit_bytes=VMEM_LIMIT),
        name="route",
    )(xn, w_query.astype(BF16), sub_keys.astype(BF16))


def _gate_piece(h_ref, coef_ref, p, e1_base, s2_ref, a2_ref, c_ref, a1_ref):
    n_e1 = TE_EXP // PEER_KEYS
    for c in range(TP_EXP // LANES):
        lc = p * (TP_EXP // LANES) + c
        ls = slice(c * LANES, (c + 1) * LANES)
        for r0 in range(0, PEER_KEYS, GATE_ROWS):
            kr = slice(r0, r0 + GATE_ROWS)
            for e0 in range(0, n_e1, 2):
                gates = [jnp.zeros((GATE_ROWS, LANES), F32), jnp.zeros((GATE_ROWS, LANES), F32)]
                for h in range(PEER_HEADS):
                    s2 = s2_ref[h, lc, kr, :]
                    a2 = a2_ref[h, lc, kr, :]
                    for k in range(2):
                        e1 = e1_base + e0 + k
                        cth = c_ref[h, lc, pl.ds(e1, 1), :]
                        a1 = a1_ref[h, lc, pl.ds(e1, 1), :]
                        gates[k] = gates[k] + jnp.where(s2 >= cth, a2, 0.0) * a1
                for k in range(2):
                    rs = slice((e0 + k) * PEER_KEYS + r0, (e0 + k) * PEER_KEYS + r0 + GATE_ROWS)
                    coef_ref[p, rs, ls] = (gates[k] * jax.nn.gelu(h_ref[p, rs, ls])).astype(BF16)


def _experts_kernel(n_steps, xn_ref, u_ref, vt_ref, s2_ref, a2_ref, c_ref, a1_ref, o_ref,
                    h0_ref, h1_ref, c0_ref, c1_ref):
    g = pl.program_id(1)
    te = TE_EXP
    n_e1 = te // PEER_KEYS
    last_e1 = (2 * n_steps - 1) * n_e1
    n_pieces = xn_ref.shape[0] // TP_EXP

    @pl.when(g == 0)
    def _():
        o_ref[...] = jnp.zeros_like(o_ref)
        h1_ref[...] = jnp.zeros_like(h1_ref)
        c0_ref[...] = jnp.zeros_like(c0_ref)

    gates = (s2_ref, a2_ref, c_ref, a1_ref)

    def half_step(u_rows, v_cols, hb_ref, hc_ref, cc_ref, ca_ref, e1_base):
        def body(p, carry):
            t0 = pl.multiple_of(p * TP_EXP, TP_EXP)
            hb_ref[p] = _nt_dot(u_ref[u_rows, :], xn_ref[pl.ds(t0, TP_EXP), :])
            _gate_piece(hc_ref, cc_ref, p, e1_base, *gates)
            o_ref[p] += jnp.dot(vt_ref[:, v_cols], ca_ref[p], preferred_element_type=F32)
            return carry

        lax.fori_loop(0, n_pieces, body, 0)

    half_step(slice(0, te), slice(0, te), h0_ref, h1_ref, c1_ref, c0_ref,
              jnp.clip((2 * g - 1) * n_e1, 0, last_e1))
    half_step(slice(te, 2 * te), slice(te, 2 * te), h1_ref, h0_ref, c0_ref, c1_ref,
              jnp.minimum(2 * g * n_e1, last_e1))


def _experts(xn, expert_u, expert_v, s2, a2, cth, a1):
    m = xn.shape[0]
    tm, te, tp = TM_EXP, TE_EXP, TP_EXP
    n_exp = expert_u.shape[0]
    n_steps = n_exp // (2 * te)
    gspec = pl.BlockSpec((PEER_HEADS, tm // LANES, PEER_KEYS, LANES), lambda i, g: (0, i, 0, 0))
    return pl.pallas_call(
        functools.partial(_experts_kernel, n_steps),
        grid=(m // tm, n_steps + 1),
        in_specs=[
            pl.BlockSpec((tm, D_MODEL), lambda i, g: (i, 0)),
            pl.BlockSpec((2 * te, D_MODEL), lambda i, g: (jnp.minimum(g, n_steps - 1), 0)),
            pl.BlockSpec((D_MODEL, 2 * te), lambda i, g: (0, jnp.maximum(g - 1, 0))),
            gspec, gspec, gspec, gspec,
        ],
        out_specs=pl.BlockSpec((tm // tp, D_MODEL, tp), lambda i, g: (i, 0, 0)),
        out_shape=jax.ShapeDtypeStruct((m // tp, D_MODEL, tp), F32),
        scratch_shapes=[pltpu.VMEM((tm // tp, te, tp), F32), pltpu.VMEM((tm // tp, te, tp), F32),
                        pltpu.VMEM((tm // tp, te, tp), BF16), pltpu.VMEM((tm // tp, te, tp), BF16)],
        compiler_params=pltpu.CompilerParams(
            dimension_semantics=("parallel", "arbitrary"), vmem_limit_bytes=VMEM_LIMIT),
        name="experts",
    )(xn, expert_u.astype(BF16), expert_v.astype(BF16).T, s2, a2, cth, a1)


def _final_kernel(h1_ref, ot_ref, p_ref, g_ref, wg_ref, wp_ref, o_ref):
    h2 = h1_ref[...] + jnp.transpose(ot_ref[0])
    hn = _rms(h2, g_ref[...]).astype(BF16)
    gate = jax.nn.sigmoid(jnp.dot(hn, wg_ref[...], preferred_element_type=F32))
    pe = jnp.dot(p_ref[...].astype(BF16), wp_ref[...], preferred_element_type=F32)
    o_ref[...] = h2 + gate * pe


def _final(h1, out_t, p2, g_ple, w_gate, w_proj):
    m = h1.shape[0]
    tm = TM_FINAL
    assert out_t.shape == (m // tm, D_MODEL, tm)
    return pl.pallas_call(
        _final_kernel,
        grid=(m // tm,),
        in_specs=[
            pl.BlockSpec((tm, D_MODEL), lambda i: (i, 0)),
            pl.BlockSpec((1, D_MODEL, tm), lambda i: (i, 0, 0)),
            pl.BlockSpec((tm, PLE_DIM), lambda i: (i, 0)),
            pl.BlockSpec((1, D_MODEL), lambda i: (0, 0)),
            pl.BlockSpec((D_MODEL, D_MODEL), lambda i: (0, 0)),
            pl.BlockSpec((PLE_DIM, D_MODEL), lambda i: (0, 0)),
        ],
        out_specs=pl.BlockSpec((tm, D_MODEL), lambda i: (i, 0)),
        out_shape=jax.ShapeDtypeStruct((m, D_MODEL), F32),
        compiler_params=pltpu.CompilerParams(dimension_semantics=("parallel",), vmem_limit_bytes=VMEM_LIMIT),
        name="final",
    )(h1, out_t, p2, g_ple, w_gate.astype(BF16), w_proj.astype(BF16))


def _layer(h2d, p2d, pos2, batch, seq, norm_mix, w_in, qk_a, qk_b, rel_bias, w_out, norm_ffn,
           peer_query, sub_keys, peer_u, peer_v, norm_ple, ple_gate, ple_proj):
    bias = _relbias_block(rel_bias)
    main, vbt, kiab, wi = _inproj(h2d, pos2, norm_mix[None, :], w_in, qk_a, qk_b)
    out_a = _mixa(main, bias, batch, seq)
    out_b = _dsa(main, vbt, kiab, wi, batch, seq)
    h1, xn = _outproj(out_a, out_b, h2d, w_out, norm_ffn[None, :])
    s2, a2, cth, a1 = _route(xn, peer_query, sub_keys)
    out_t = _experts(xn, peer_u, peer_v, s2, a2, cth, a1)
    return _final(h1, out_t, p2d, norm_ple[None, :], ple_gate, ple_proj)


def kernel(x, p, positions, norm_mix, w_in, qk_norm_a, qk_norm_b, rel_bias, w_out, norm_ffn, peer_query,
           peer_sub_keys, peer_u, peer_v, norm_ple, ple_gate, ple_proj):
    batch, seq, d = x.shape
    assert d == D_MODEL and seq % TM_PROJ == 0 and (batch * seq) % TM_PROJ == 0
    h = x.reshape(batch * seq, d)
    pos2 = positions.reshape(batch * seq, 1).astype(jnp.int32)
    for i in range(p.shape[0]):
        h = _layer(h, p[i].reshape(batch * seq, PLE_DIM), pos2, batch, seq, norm_mix[i], w_in[i],
                   qk_norm_a[i], qk_norm_b[i], rel_bias[i], w_out[i], norm_ffn[i], peer_query[i],
                   peer_sub_keys[i], peer_u[i], peer_v[i], norm_ple[i], ple_gate[i], ple_proj[i])
    return h.reshape(batch, seq, d)
```

```python
import functools

import numpy as np
import jax
import jax.numpy as jnp
from jax import lax
from jax.experimental import pallas as pl
from jax.experimental.pallas import tpu as pltpu

F32 = jnp.float32
BF16 = jnp.bfloat16

D_MODEL = 2048
HEAD_DIM = 128
N_HEADS = 8
WIDTH = N_HEADS * HEAD_DIM
CHUNK = 64
LEFT_CHUNKS = 8
MAX_REL = 128
IDX_HEADS = 16
IDX_DIM = 64
TOPK_MAX = 256
ROPE_THETA = 500000.0
ROPE_FRACTION = 4
PEER_HEADS = 8
PEER_KEYS = 128
PEER_TOPK = 16
PLE_DIM = 256
EPS = 1e-6

LANES = 128
SUBLANES = 8
VMEM_LIMIT = 56 * 1024 * 1024

NEG = -1e30

TM_PROJ = 512
QA_BLOCK = 256
QA_WIN = 768
TQ_DSA = 256
KSUB_DSA = 128
BITS_PER_CHECK = 4
TM_ROUTE = 512
TM_EXP = 512
TE_EXP = 512
TM_FINAL = 256
GATE_ROWS = 64
ANCHOR_SPAN_PCT = 100
TP_EXP = 256

_NT = (((1,), (1,)), ((), ()))


def _nt_dot(a, b):
    return lax.dot_general(a, b, _NT, preferred_element_type=F32)


def _rms(x, g):
    return x * lax.rsqrt(jnp.mean(x * x, axis=-1, keepdims=True) + EPS) * g


def _rope_lane_table(group, half):
    lane = np.arange(LANES)
    m = lane % group
    tab = np.zeros((SUBLANES, LANES), np.float32)
    tab[0] = np.where(m < 2 * half, (m % half) / half, 0.0)
    tab[1] = (m < half)
    tab[2] = (m >= half) & (m < 2 * half)
    tab[3] = (m < 2 * half)
    return tab


def _rope_coeffs(pos_f, tab_ref, out_ref):
    inv_freq = tab_ref[3:4, :] / jnp.power(jnp.float32(ROPE_THETA), tab_ref[0:1, :])
    ang = pos_f * inv_freq
    s = jnp.sin(ang)
    out_ref[0] = jnp.cos(ang)
    out_ref[1] = -s * tab_ref[1:2, :]
    out_ref[2] = s * tab_ref[2:3, :]


def _apply_rope(x, coef_ref, half):
    up = pltpu.roll(x, LANES - half, 1)
    dn = pltpu.roll(x, half, 1)
    return x * coef_ref[0] + up * coef_ref[1] + dn * coef_ref[2]


def _relbias_kernel(u_ref, o_ref):
    x = jnp.broadcast_to(u_ref[0, 0:1, :], (QA_BLOCK, 1024))
    y = pltpu.roll(x, 0, 1, stride=1, stride_axis=0)[:, :QA_WIN]
    qc = lax.broadcasted_iota(jnp.int32, (QA_BLOCK, QA_WIN), 0) // CHUNK
    kc = lax.broadcasted_iota(jnp.int32, (QA_BLOCK, QA_WIN), 1) // CHUNK
    band = (kc >= qc) & (kc <= qc + LEFT_CHUNKS)
    o_ref[0] = jnp.where(band, y, NEG)


def _relbias_block(rel_bias):
    p = np.arange(1024)
    off = np.where(p < QA_WIN, p, p - 1024)
    idx = np.clip(LEFT_CHUNKS * CHUNK - off, -MAX_REL, MAX_REL) + MAX_REL
    u = jnp.take(rel_bias.astype(F32), jnp.asarray(idx, jnp.int32), axis=1)
    u = jnp.broadcast_to(u[:, None, :], (N_HEADS, SUBLANES, 1024))
    return pl.pallas_call(
        _relbias_kernel,
        grid=(N_HEADS,),
        in_specs=[pl.BlockSpec((1, SUBLANES, 1024), lambda h: (h, 0, 0))],
        out_specs=pl.BlockSpec((1, QA_BLOCK, QA_WIN), lambda h: (h, 0, 0)),
        out_shape=jax.ShapeDtypeStruct((N_HEADS, QA_BLOCK, QA_WIN), F32),
        name="relbias",
    )(u)


R_QA, R_KA, R_VA, R_QB, R_KB, R_QI = range(6)


def _inproj_kernel(x_ref, g_ref, w_ref, wvt_ref, wtail_ref, qkg_ref, pos_ref, tab128_ref, tab64_ref,
                   main_ref, vbt_ref, kiab_ref, wi_ref,
                   hn_ref, acc_ref, rope128_ref, rope64_ref):
    j = pl.program_id(1)
    tm = x_ref.shape[0]

    @pl.when(j == 0)
    def _():
        hn_ref[...] = _rms(x_ref[...], g_ref[...]).astype(BF16)
        pos_f = pos_ref[...].astype(F32)
        _rope_coeffs(pos_f, tab128_ref, rope128_ref)
        _rope_coeffs(pos_f, tab64_ref, rope64_ref)
        vt = _nt_dot(wvt_ref[...], hn_ref[...]).astype(BF16)
        for t in range(tm // TQ_DSA):
            vbt_ref[t] = vt[:, t * TQ_DSA:(t + 1) * TQ_DSA]
        tail = jnp.dot(hn_ref[...], wtail_ref[...], preferred_element_type=F32)
        kk = _apply_rope(tail[:, :LANES], rope64_ref, IDX_DIM // ROPE_FRACTION // 2)
        lane = lax.broadcasted_iota(jnp.int32, (tm, LANES), 1)
        kiab_ref[:, :LANES] = jnp.where(lane < IDX_DIM, kk, 0.0).astype(BF16)
        kiab_ref[:, LANES:] = jnp.where(lane >= IDX_DIM, kk, 0.0).astype(BF16)
        wi_ref[...] = tail[:, LANES:]

    acc_ref[...] = jnp.dot(hn_ref[...], w_ref[...], preferred_element_type=F32)

    def heads(fn):
        for h in range(N_HEADS):
            sl = slice(h * HEAD_DIM, (h + 1) * HEAD_DIM)
            main_ref[:, sl] = fn(acc_ref[:, sl]).astype(BF16)

    half_b = HEAD_DIM // ROPE_FRACTION // 2

    @pl.when(j == R_QA)
    def _():
        heads(lambda a: _rms(a, qkg_ref[0:1, :]))

    @pl.when(j == R_KA)
    def _():
        heads(lambda a: _rms(a, qkg_ref[1:2, :]))

    @pl.when(j == R_VA)
    def _():
        main_ref[...] = acc_ref[...].astype(BF16)

    @pl.when(j == R_QB)
    def _():
        heads(lambda a: _apply_rope(_rms(a, qkg_ref[2:3, :]), rope128_ref, half_b))

    @pl.when(j == R_KB)
    def _():
        heads(lambda a: _apply_rope(_rms(a, qkg_ref[3:4, :]), rope128_ref, half_b))

    @pl.when(j == R_QI)
    def _():
        heads(lambda a: _apply_rope(a, rope64_ref, IDX_DIM // ROPE_FRACTION // 2))


def _inproj(x2, pos2, g_mix, w_in, qk_a, qk_b):
    m = x2.shape[0]
    tm = TM_PROJ
    w = w_in.astype(BF16)
    o = [0, WIDTH, 2 * WIDTH, 3 * WIDTH, 4 * WIDTH, 5 * WIDTH, 6 * WIDTH, 7 * WIDTH]
    w_main = jnp.concatenate([w[:, o[0]:o[5]], w[:, o[6]:o[7]]], axis=1)
    w_vt = w[:, o[5]:o[6]].T
    w_ki = w[:, o[7]:o[7] + IDX_DIM]
    w_wi = w[:, o[7] + IDX_DIM:o[7] + IDX_DIM + IDX_HEADS]
    w_tail = jnp.concatenate([w_ki, w_ki, w_wi, jnp.zeros((D_MODEL, LANES - IDX_HEADS), BF16)], axis=1)
    qkg = jnp.concatenate([qk_a, qk_b, jnp.zeros((4, HEAD_DIM), F32)], axis=0).astype(F32)
    tab128 = jnp.asarray(_rope_lane_table(HEAD_DIM, HEAD_DIM // ROPE_FRACTION // 2))
    tab64 = jnp.asarray(_rope_lane_table(IDX_DIM, IDX_DIM // ROPE_FRACTION // 2))
    res = lambda shape: pl.BlockSpec(shape, lambda i, j: (0,) * len(shape))
    return pl.pallas_call(
        _inproj_kernel,
        grid=(m // tm, 6),
        in_specs=[
            pl.BlockSpec((tm, D_MODEL), lambda i, j: (i, 0)),
            res((1, D_MODEL)),
            pl.BlockSpec((D_MODEL, WIDTH), lambda i, j: (0, j)),
            res((WIDTH, D_MODEL)),
            res((D_MODEL, 2 * LANES)),
            res((SUBLANES, HEAD_DIM)),
            pl.BlockSpec((tm, 1), lambda i, j: (i, 0)),
            res((SUBLANES, LANES)),
            res((SUBLANES, LANES)),
        ],
        out_specs=[
            pl.BlockSpec((tm, WIDTH), lambda i, j: (i, j)),
            pl.BlockSpec((tm // TQ_DSA, WIDTH, TQ_DSA), lambda i, j: (i, 0, 0)),
            pl.BlockSpec((tm, 2 * LANES), lambda i, j: (i, 0)),
            pl.BlockSpec((tm, LANES), lambda i, j: (i, 0)),
        ],
        out_shape=[
            jax.ShapeDtypeStruct((m, 6 * WIDTH), BF16),
            jax.ShapeDtypeStruct((m // TQ_DSA, WIDTH, TQ_DSA), BF16),
            jax.ShapeDtypeStruct((m, 2 * LANES), BF16),
            jax.ShapeDtypeStruct((m, LANES), F32),
        ],
        scratch_shapes=[
            pltpu.VMEM((tm, D_MODEL), BF16),
            pltpu.VMEM((tm, WIDTH), F32),
            pltpu.VMEM((3, tm, LANES), F32),
            pltpu.VMEM((3, tm, LANES), F32),
        ],
        compiler_params=pltpu.CompilerParams(
            dimension_semantics=("parallel", "arbitrary"), vmem_limit_bytes=VMEM_LIMIT),
        name="inproj",
    )(x2, g_mix, w_main, w_vt, w_tail, qkg, pos2, tab128, tab64)


def _mixa_kernel(q_ref, k0_ref, k1_ref, k2_ref, v0_ref, v1_ref, v2_ref, bias_ref, o_ref):
    qb = pl.program_id(1)
    scale = HEAD_DIM ** -0.5
    k_refs = (k0_ref, k1_ref, k2_ref)
    v_refs = (v0_ref, v1_ref, v2_ref)
    pen = (jnp.where(qb >= 2, 0.0, NEG).astype(F32), jnp.where(qb >= 1, 0.0, NEG).astype(F32), None)
    for h in range(N_HEADS):
        sl = slice(h * HEAD_DIM, (h + 1) * HEAD_DIM)
        q = q_ref[:, sl]
        s = []
        for m in range(3):
            sm = _nt_dot(q, k_refs[m][:, sl]) * scale + bias_ref[h, :, m * QA_BLOCK:(m + 1) * QA_BLOCK]
            if pen[m] is not None:
                sm = sm + pen[m]
            s.append(sm)
        mx = jnp.maximum(jnp.maximum(jnp.max(s[0], axis=-1, keepdims=True), jnp.max(s[1], axis=-1, keepdims=True)),
                         jnp.max(s[2], axis=-1, keepdims=True))
        p = [jnp.exp(sm - mx) for sm in s]
        den = (jnp.sum(p[0], axis=-1, keepdims=True) + jnp.sum(p[1], axis=-1, keepdims=True)
               + jnp.sum(p[2], axis=-1, keepdims=True))
        o = jnp.dot(p[0].astype(BF16), v_refs[0][:, sl], preferred_element_type=F32)
        o = o + jnp.dot(p[1].astype(BF16), v_refs[1][:, sl], preferred_element_type=F32)
        o = o + jnp.dot(p[2].astype(BF16), v_refs[2][:, sl], preferred_element_type=F32)
        o_ref[:, sl] = (o / den).astype(BF16)


def _mixa(main, bias, batch, seq):
    m = main.shape[0]
    nqb = seq // QA_BLOCK

    def kv_spec(region, back):
        return pl.BlockSpec((QA_BLOCK, WIDTH),
                            lambda b, i: (b * nqb + jnp.maximum(i - back, 0), region))

    return pl.pallas_call(
        _mixa_kernel,
        grid=(batch, nqb),
        in_specs=[
            pl.BlockSpec((QA_BLOCK, WIDTH), lambda b, i: (b * nqb + i, R_QA)),
            kv_spec(R_KA, 2), kv_spec(R_KA, 1), kv_spec(R_KA, 0),
            kv_spec(R_VA, 2), kv_spec(R_VA, 1), kv_spec(R_VA, 0),
            pl.BlockSpec((N_HEADS, QA_BLOCK, QA_WIN), lambda b, i: (0, 0, 0)),
        ],
        out_specs=pl.BlockSpec((QA_BLOCK, WIDTH), lambda b, i: (b * nqb + i, 0)),
        out_shape=jax.ShapeDtypeStruct((m, WIDTH), BF16),
        compiler_params=pltpu.CompilerParams(
            dimension_semantics=("parallel", "parallel"), vmem_limit_bytes=VMEM_LIMIT),
        name="mixa",
    )(main, main, main, main, main, main, main, bias)


def _key_to_f32(t):
    return pltpu.bitcast(t ^ ((t >> 31) & jnp.int32(0x7FFFFFFF)), F32)


def _dsa_kernel(topk, qb_ref, kb_ref, vbt_ref, qi_ref, kiab_ref, wi_ref, o_ref,
                sc_ref, bias_ref, acc_ref, m_ref, l_ref):
    i = pl.program_id(1)
    tq = TQ_DSA
    nkt = i + 1
    c2 = HEAD_DIM ** -0.5 * 1.4426950408889634

    w_t = jnp.transpose(wi_ref[...]) * (IDX_HEADS ** -0.5 * IDX_DIM ** -0.5)
    qchunk = (i * tq + lax.broadcasted_iota(jnp.int32, (1, tq), 1)) // CHUNK
    row = lax.broadcasted_iota(jnp.int32, (tq, tq), 0)

    def admissible(kt):
        return (kt * tq + row) // CHUNK <= qchunk

    def idx_body(kt, carry):
        r0 = pl.multiple_of(kt * tq, tq)
        ka = kiab_ref[pl.ds(r0, tq), :LANES]
        kb = kiab_ref[pl.ds(r0, tq), LANES:]
        acc = jnp.zeros((tq, tq), F32)
        for p in range(IDX_HEADS // 2):
            qp = qi_ref[:, p * LANES:(p + 1) * LANES]
            acc = acc + w_t[2 * p:2 * p + 1, :] * jnp.maximum(_nt_dot(ka, qp), 0.0)
            acc = acc + w_t[2 * p + 1:2 * p + 2, :] * jnp.maximum(_nt_dot(kb, qp), 0.0)
        sc_ref[pl.ds(r0, tq), :] = jnp.where(admissible(kt), acc, -jnp.inf)
        return carry

    lax.fori_loop(0, nkt, idx_body, 0)

    n_adm = (qchunk + 1) * CHUNK
    need = n_adm > topk

    def count_ge(c_f):
        def cnt_body(kt, c):
            r0 = pl.multiple_of(kt * tq, tq)
            ge = jnp.where(sc_ref[pl.ds(r0, tq), :] >= c_f, 1, 0).astype(jnp.int32)
            return c + jnp.sum(ge.reshape(tq // SUBLANES, SUBLANES, tq), axis=0)

        c8 = lax.fori_loop(0, nkt, cnt_body, jnp.zeros((SUBLANES, tq), jnp.int32))
        return jnp.sum(c8, axis=0, keepdims=True)

    def bit_cond(carry):
        b, _, cnt = carry
        todo = jnp.max(jnp.where(need & (cnt != topk), 1, 0))
        return (b >= 0) & (todo > 0)

    def bit_body(carry):
        b, thr, cnt = carry
        for _ in range(BITS_PER_CHECK):
            cand = thr + lax.shift_left(jnp.int32(1), b)
            n = count_ge(_key_to_f32(cand))
            ok = n >= topk
            b, thr, cnt = b - 1, jnp.where(ok, cand, thr), jnp.where(ok, n, cnt)
        return b, thr, cnt

    init = (jnp.int32(31), jnp.full((1, tq), jnp.iinfo(jnp.int32).min, jnp.int32),
            jnp.full((1, tq), jnp.iinfo(jnp.int32).max, jnp.int32))
    _, thr, _ = lax.while_loop(bit_cond, bit_body, init)
    thr_f = _key_to_f32(thr)

    def mask_body(kt, carry):
        r0 = pl.multiple_of(kt * tq, tq)
        sel = admissible(kt) & ((sc_ref[pl.ds(r0, tq), :] >= thr_f) | jnp.logical_not(need))
        bias_ref[pl.ds(r0, tq), :] = jnp.where(sel, 0.0, NEG).astype(F32)
        return carry

    lax.fori_loop(0, nkt, mask_body, 0)

    acc_ref[...] = jnp.zeros_like(acc_ref)
    m_ref[...] = jnp.full(m_ref.shape, NEG, F32)
    l_ref[...] = jnp.zeros_like(l_ref)

    def att_body(kt, carry):
        for half in range(tq // KSUB_DSA):
            r0 = pl.multiple_of(kt * tq + half * KSUB_DSA, KSUB_DSA)
            ks = slice(half * KSUB_DSA, (half + 1) * KSUB_DSA)
            for h in range(N_HEADS):
                sl = slice(h * HEAD_DIM, (h + 1) * HEAD_DIM)
                s = _nt_dot(kb_ref[pl.ds(r0, KSUB_DSA), sl], qb_ref[:, sl]) + bias_ref[pl.ds(r0, KSUB_DSA), :]
                m_old = m_ref[h:h + 1, :]
                m_new = jnp.maximum(m_old, jnp.max(s, axis=0, keepdims=True))
                alpha = jnp.exp2((m_old - m_new) * c2)
                p = jnp.exp2((s - m_new) * c2)
                l_ref[h:h + 1, :] = alpha * l_ref[h:h + 1, :] + jnp.sum(p, axis=0, keepdims=True)
                m_ref[h:h + 1, :] = m_new
                pv = jnp.dot(vbt_ref[kt, sl, ks], p.astype(BF16), preferred_element_type=F32)
                acc_ref[h] = alpha * acc_ref[h] + pv
        return carry

    lax.fori_loop(0, nkt, att_body, 0)
    for h in range(N_HEADS):
        sl = slice(h * HEAD_DIM, (h + 1) * HEAD_DIM)
        o_ref[:, sl] = jnp.transpose(acc_ref[h] / l_ref[h:h + 1, :]).astype(BF16)


def _dsa(main, vbt, kiab, wi, batch, seq):
    m = main.shape[0]
    tq = TQ_DSA
    nq = seq // tq
    topk = min(TOPK_MAX, seq // 4)
    return pl.pallas_call(
        functools.partial(_dsa_kernel, topk),
        grid=(batch, nq),
        in_specs=[
            pl.BlockSpec((tq, WIDTH), lambda b, i: (b * nq + i, R_QB)),
            pl.BlockSpec((seq, WIDTH), lambda b, i: (b, R_KB)),
            pl.BlockSpec((nq, WIDTH, tq), lambda b, i: (b, 0, 0)),
            pl.BlockSpec((tq, WIDTH), lambda b, i: (b * nq + i, R_QI)),
            pl.BlockSpec((seq, 2 * LANES), lambda b, i: (b, 0)),
            pl.BlockSpec((tq, LANES), lambda b, i: (b * nq + i, 0)),
        ],
        out_specs=pl.BlockSpec((tq, WIDTH), lambda b, i: (b * nq + i, 0)),
        out_shape=jax.ShapeDtypeStruct((m, WIDTH), BF16),
        scratch_shapes=[pltpu.VMEM((seq, tq), F32), pltpu.VMEM((seq, tq), F32),
                        pltpu.VMEM((N_HEADS, HEAD_DIM, tq), F32),
                        pltpu.VMEM((N_HEADS, tq), F32), pltpu.VMEM((N_HEADS, tq), F32)],
        compiler_params=pltpu.CompilerParams(
            dimension_semantics=("parallel", "arbitrary"), vmem_limit_bytes=VMEM_LIMIT),
        name="dsa",
    )(main, main, vbt, main, kiab, wi)


def _outproj_kernel(oa_ref, ob_ref, x_ref, wo_ref, g_ref, h1_ref, xn_ref):
    acc = jnp.dot(oa_ref[...], wo_ref[:WIDTH, :], preferred_element_type=F32)
    acc = acc + jnp.dot(ob_ref[...], wo_ref[WIDTH:, :], preferred_element_type=F32)
    h1 = x_ref[...] + acc
    h1_ref[...] = h1
    xn_ref[...] = _rms(h1, g_ref[...]).astype(BF16)


def _outproj(out_a, out_b, x2, w_out, g_ffn):
    m = x2.shape[0]
    tm = TM_PROJ
    return pl.pallas_call(
        _outproj_kernel,
        grid=(m // tm,),
        in_specs=[
            pl.BlockSpec((tm, WIDTH), lambda i: (i, 0)),
            pl.BlockSpec((tm, WIDTH), lambda i: (i, 0)),
            pl.BlockSpec((tm, D_MODEL), lambda i: (i, 0)),
            pl.BlockSpec((2 * WIDTH, D_MODEL), lambda i: (0, 0)),
            pl.BlockSpec((1, D_MODEL), lambda i: (0, 0)),
        ],
        out_specs=[pl.BlockSpec((tm, D_MODEL), lambda i: (i, 0)), pl.BlockSpec((tm, D_MODEL), lambda i: (i, 0))],
        out_shape=[jax.ShapeDtypeStruct((m, D_MODEL), F32), jax.ShapeDtypeStruct((m, D_MODEL), BF16)],
        compiler_params=pltpu.CompilerParams(dimension_semantics=("parallel",), vmem_limit_bytes=VMEM_LIMIT),
        name="outproj",
    )(out_a, out_b, x2, w_out.astype(BF16), g_ffn)


def _merge_exchange_pairs(n):
    t = (n - 1).bit_length()
    pairs = []
    p = 1 << (t - 1)
    while p > 0:
        q, r, d = 1 << (t - 1), 0, p
        while True:
            pairs.extend((i, i + d) for i in range(n - d) if (i & p) == r)
            if q == p:
                break
            d, q, r = q - p, q >> 1, p
        p >>= 1
    return pairs


_SORT16 = _merge_exchange_pairs(PEER_TOPK)


def _tmax(a, b):
    if a is None:
        return b
    if b is None:
        return a
    return jnp.maximum(a, b)


def _cmp_exchange(x, i, j):
    a, b = x[i], x[j]
    if b is None:
        return
    if a is None:
        x[i], x[j] = b, None
        return
    x[i], x[j] = jnp.maximum(a, b), jnp.minimum(a, b)


def _sort16_desc(x):
    for i, j in _SORT16:
        _cmp_exchange(x, i, j)


def _bitonic16_desc(x):
    for d in (8, 4, 2, 1):
        for i in range(PEER_TOPK):
            if not i & d:
                _cmp_exchange(x, i, i + d)


def _merge_sublanes(x, shift, sort=True):
    y = [None if v is None else pltpu.roll(v, shift, 0) for v in x]
    out = [_tmax(x[i], y[PEER_TOPK - 1 - i]) for i in range(PEER_TOPK)]
    if sort:
        _bitonic16_desc(out)
    return out


def _top16(tiles):
    x = list(tiles)
    _sort16_desc(x)
    for shift in (4, 2, 1):
        x = _merge_sublanes(x, shift)
    return x


def _pair_threshold(v1, v2):
    sub = lax.broadcasted_iota(jnp.int32, v1[0].shape, 0)

    def diag(vals):
        d = vals[0]
        for s in range(1, SUBLANES):
            d = jnp.where(sub == s, vals[s], d)
        return d

    d1a, d1b = diag(v1[:8]), diag(v1[8:])
    d2a, d2b = diag(v2[:8]), diag(v2[8:])
    cand = [d1a + v2[0], d1b + v2[0], d1a + v2[1], d1a + v2[2], d1a + v2[3], v1[0] + d2b]
    cand += [jnp.where(sub < 4, -jnp.inf, v1[r] + d2a) for r in range(3)]
    x = cand + [None] * (PEER_TOPK - len(cand))
    _sort16_desc(x)
    x = _merge_sublanes(x, 4)
    x = _merge_sublanes(x, 2)
    x = _merge_sublanes(x, 1, sort=False)
    tau = x[0]
    for v in x[1:]:
        tau = jnp.minimum(tau, v)
    return tau


def _sublane_total(x):
    for shift in (4, 2, 1):
        x = x + pltpu.roll(x, shift, 0)
    return x


def _route_kernel(xn_ref, wq_ref, sk_ref, s2_ref, a2_ref, c_ref, a1_ref, qv_ref):
    tm = xn_ref.shape[0]
    n_t = PEER_KEYS // SUBLANES
    qv = jnp.dot(xn_ref[...], wq_ref[...], preferred_element_type=F32)
    for c in range(2 * PEER_HEADS):
        qv_ref[c] = qv[:, c * PEER_KEYS:(c + 1) * PEER_KEYS].astype(BF16)

    def chunk(h, lc):
        t0 = pl.multiple_of(lc * LANES, LANES)
        s1 = _nt_dot(sk_ref[0], qv_ref[2 * h, pl.ds(t0, LANES), :])
        s2 = _nt_dot(sk_ref[1], qv_ref[2 * h + 1, pl.ds(t0, LANES), :])
        s1 = [s1[j * SUBLANES:(j + 1) * SUBLANES, :] for j in range(n_t)]
        s2 = [s2[j * SUBLANES:(j + 1) * SUBLANES, :] for j in range(n_t)]
        v1 = _top16(s1)
        v2 = _top16(s2)
        tau = _pair_threshold(v1, v2)
        pre = []
        for q in range(PEER_TOPK):
            e = jnp.exp(v2[q] - v2[0])
            pre.append(e if q == 0 else pre[-1] + e)
        a1, cth, z = [], [], None
        for j in range(n_t):
            c_j = jnp.full(s1[j].shape, jnp.inf, F32)
            m_j = jnp.zeros(s1[j].shape, F32)
            for q in range(PEER_TOPK):
                ok = s1[j] + v2[q] >= tau
                c_j = jnp.where(ok, v2[q], c_j)
                m_j = jnp.where(ok, pre[q], m_j)
            a_j = jnp.exp(s1[j] - v1[0])
            a1.append(a_j)
            cth.append(c_j)
            z = a_j * m_j if z is None else z + a_j * m_j
        z = _sublane_total(z)
        for j in range(n_t):
            rs = slice(j * SUBLANES, (j + 1) * SUBLANES)
            s2_ref[h, lc, rs, :] = s2[j]
            a2_ref[h, lc, rs, :] = jnp.exp(s2[j] - v2[0])
            c_ref[h, lc, rs, :] = cth[j]
            a1_ref[h, lc, rs, :] = a1[j] / z

    def head_body(h, carry):
        def chunk_body(lc, c):
            chunk(h, lc)
            return c

        return lax.fori_loop(0, tm // LANES, chunk_body, carry)

    lax.fori_loop(0, PEER_HEADS, head_body, 0)


def _route(xn, w_query, sub_keys):
    m = xn.shape[0]
    tm = TM_ROUTE
    spec = pl.BlockSpec((PEER_HEADS, tm // LANES, PEER_KEYS, LANES), lambda i: (0, i, 0, 0))
    shp = jax.ShapeDtypeStruct((PEER_HEADS, m // LANES, PEER_KEYS, LANES), F32)
    return pl.pallas_call(
        _route_kernel,
        grid=(m // tm,),
        in_specs=[
            pl.BlockSpec((tm, D_MODEL), lambda i: (i, 0)),
            pl.BlockSpec((D_MODEL, 2 * PEER_HEADS * PEER_KEYS), lambda i: (0, 0)),
            pl.BlockSpec((2, PEER_KEYS, PEER_KEYS), lambda i: (0, 0, 0)),
        ],
        out_specs=[spec, spec, spec, spec],
        out_shape=[shp, shp, shp, shp],
        scratch_shapes=[pltpu.VMEM((2 * PEER_HEADS, tm, PEER_KEYS), BF16)],
        compiler_params=pltpu.CompilerParams(dimension_semantics=("parallel",), vmem_limit_bytes=VMEM_LIMIT),
        name="route",
    )(xn, w_query.astype(BF16), sub_keys.astype(BF16))


def _zero_row_from(x):
    bits = pltpu.bitcast(x, jnp.int32)
    z = lax.shift_right_logical(lax.shift_right_logical(bits, 16), 16)
    return z[0:1, :].astype(F32)


def _gate_piece(h_ref, coef_ref, p, e1_base, s2_ref, a2_ref, c_ref, a1_ref, anchors):
    n_e1 = TE_EXP // PEER_KEYS
    unit = 0
    for c in range(TP_EXP // LANES):
        lc = p * (TP_EXP // LANES) + c
        ls = slice(c * LANES, (c + 1) * LANES)
        for r0 in range(0, PEER_KEYS, GATE_ROWS):
            kr = slice(r0, r0 + GATE_ROWS)
            for e0 in range(0, n_e1, 2):
                gates = [jnp.zeros((GATE_ROWS, LANES), F32), jnp.zeros((GATE_ROWS, LANES), F32)]
                for h in range(PEER_HEADS):
                    anchor = anchors[unit * PEER_HEADS + h]
                    s2 = s2_ref[h, lc, kr, :]
                    a2 = a2_ref[h, lc, kr, :]
                    for k in range(2):
                        e1 = e1_base + e0 + k
                        cth = c_ref[h, lc, pl.ds(e1, 1), :] + anchor
                        a1 = a1_ref[h, lc, pl.ds(e1, 1), :]
                        gates[k] = gates[k] + jnp.where(s2 >= cth, a2, 0.0) * a1
                for k in range(2):
                    rs = slice((e0 + k) * PEER_KEYS + r0, (e0 + k) * PEER_KEYS + r0 + GATE_ROWS)
                    act = jax.nn.gelu(h_ref[p, rs, ls] + anchor)
                    coef_ref[p, rs, ls] = (gates[k] * act).astype(BF16)
                unit += 1


def _experts_kernel(n_steps, xn_ref, u_ref, vt_ref, s2_ref, a2_ref, c_ref, a1_ref, o_ref,
                    h0_ref, h1_ref, c0_ref, c1_ref):
    g = pl.program_id(1)
    te = TE_EXP
    n_e1 = te // PEER_KEYS
    last_e1 = (2 * n_steps - 1) * n_e1
    n_pieces = xn_ref.shape[0] // TP_EXP
    n_anchor = (TP_EXP // LANES) * (PEER_KEYS // GATE_ROWS) * (n_e1 // 2) * PEER_HEADS

    @pl.when(g == 0)
    def _():
        o_ref[...] = jnp.zeros_like(o_ref)
        h1_ref[...] = jnp.zeros_like(h1_ref)
        c0_ref[...] = jnp.zeros_like(c0_ref)

    gates = (s2_ref, a2_ref, c_ref, a1_ref)

    def half_step(u_rows, v_cols, hb_ref, hc_ref, cc_ref, ca_ref, e1_base):
        def body(p, carry):
            t0 = pl.multiple_of(p * TP_EXP, TP_EXP)
            hb_ref[p] = _nt_dot(u_ref[u_rows, :], xn_ref[pl.ds(t0, TP_EXP), :])
            res = jnp.dot(vt_ref[:, v_cols], ca_ref[p], preferred_element_type=F32)
            o_ref[p] += res
            step = (D_MODEL * ANCHOR_SPAN_PCT // 100) // n_anchor // SUBLANES * SUBLANES
            anchors = [_zero_row_from(res[k * step:k * step + SUBLANES, 0:LANES]) for k in range(n_anchor)]
            _gate_piece(hc_ref, cc_ref, p, e1_base, *gates, anchors)
            return carry

        lax.fori_loop(0, n_pieces, body, 0)

    half_step(slice(0, te), slice(0, te), h0_ref, h1_ref, c1_ref, c0_ref,
              jnp.clip((2 * g - 1) * n_e1, 0, last_e1))
    half_step(slice(te, 2 * te), slice(te, 2 * te), h1_ref, h0_ref, c0_ref, c1_ref,
              jnp.minimum(2 * g * n_e1, last_e1))


def _experts(xn, expert_u, expert_v, s2, a2, cth, a1):
    m = xn.shape[0]
    tm, te, tp = TM_EXP, TE_EXP, TP_EXP
    n_exp = expert_u.shape[0]
    n_steps = n_exp // (2 * te)
    gspec = pl.BlockSpec((PEER_HEADS, tm // LANES, PEER_KEYS, LANES), lambda i, g: (0, i, 0, 0))
    return pl.pallas_call(
        functools.partial(_experts_kernel, n_steps),
        grid=(m // tm, n_steps + 1),
        in_specs=[
            pl.BlockSpec((tm, D_MODEL), lambda i, g: (i, 0)),
            pl.BlockSpec((2 * te, D_MODEL), lambda i, g: (jnp.minimum(g, n_steps - 1), 0)),
            pl.BlockSpec((D_MODEL, 2 * te), lambda i, g: (0, jnp.maximum(g - 1, 0))),
            gspec, gspec, gspec, gspec,
        ],
        out_specs=pl.BlockSpec((tm // tp, D_MODEL, tp), lambda i, g: (i, 0, 0)),
        out_shape=jax.ShapeDtypeStruct((m // tp, D_MODEL, tp), F32),
        scratch_shapes=[pltpu.VMEM((tm // tp, te, tp), F32), pltpu.VMEM((tm // tp, te, tp), F32),
                        pltpu.VMEM((tm // tp, te, tp), BF16), pltpu.VMEM((tm // tp, te, tp), BF16)],
        compiler_params=pltpu.CompilerParams(
            dimension_semantics=("parallel", "arbitrary"), vmem_limit_bytes=VMEM_LIMIT),
        name="experts",
    )(xn, expert_u.astype(BF16), expert_v.astype(BF16).T, s2, a2, cth, a1)


def _final_kernel(h1_ref, ot_ref, p_ref, g_ref, wg_ref, wp_ref, o_ref):
    h2 = h1_ref[...] + jnp.transpose(ot_ref[0])
    hn = _rms(h2, g_ref[...]).astype(BF16)
    gate = jax.nn.sigmoid(jnp.dot(hn, wg_ref[...], preferred_element_type=F32))
    pe = jnp.dot(p_ref[...].astype(BF16), wp_ref[...], preferred_element_type=F32)
    o_ref[...] = h2 + gate * pe


def _final(h1, out_t, p2, g_ple, w_gate, w_proj):
    m = h1.shape[0]
    tm = TM_FINAL
    assert out_t.shape == (m // tm, D_MODEL, tm)
    return pl.pallas_call(
        _final_kernel,
        grid=(m // tm,),
        in_specs=[
            pl.BlockSpec((tm, D_MODEL), lambda i: (i, 0)),
            pl.BlockSpec((1, D_MODEL, tm), lambda i: (i, 0, 0)),
            pl.BlockSpec((tm, PLE_DIM), lambda i: (i, 0)),
            pl.BlockSpec((1, D_MODEL), lambda i: (0, 0)),
            pl.BlockSpec((D_MODEL, D_MODEL), lambda i: (0, 0)),
            pl.BlockSpec((PLE_DIM, D_MODEL), lambda i: (0, 0)),
        ],
        out_specs=pl.BlockSpec((tm, D_MODEL), lambda i: (i, 0)),
        out_shape=jax.ShapeDtypeStruct((m, D_MODEL), F32),
        compiler_params=pltpu.CompilerParams(dimension_semantics=("parallel",), vmem_limit_bytes=VMEM_LIMIT),
        name="final",
    )(h1, out_t, p2, g_ple, w_gate.astype(BF16), w_proj.astype(BF16))


def _layer(h2d, p2d, pos2, batch, seq, norm_mix, w_in, qk_a, qk_b, rel_bias, w_out, norm_ffn,
           peer_query, sub_keys, peer_u, peer_v, norm_ple, ple_gate, ple_proj):
    bias = _relbias_block(rel_bias)
    main, vbt, kiab, wi = _inproj(h2d, pos2, norm_mix[None, :], w_in, qk_a, qk_b)
    out_a = _mixa(main, bias, batch, seq)
    out_b = _dsa(main, vbt, kiab, wi, batch, seq)
    h1, xn = _outproj(out_a, out_b, h2d, w_out, norm_ffn[None, :])
    s2, a2, cth, a1 = _route(xn, peer_query, sub_keys)
    out_t = _experts(xn, peer_u, peer_v, s2, a2, cth, a1)
    return _final(h1, out_t, p2d, norm_ple[None, :], ple_gate, ple_proj)


def kernel(x, p, positions, norm_mix, w_in, qk_norm_a, qk_norm_b, rel_bias, w_out, norm_ffn, peer_query,
           peer_sub_keys, peer_u, peer_v, norm_ple, ple_gate, ple_proj):
    batch, seq, d = x.shape
    assert d == D_MODEL and seq % TM_PROJ == 0 and (batch * seq) % TM_PROJ == 0
    h = x.reshape(batch * seq, d)
    pos2 = positions.reshape(batch * seq, 1).astype(jnp.int32)
    for i in range(p.shape[0]):
        h = _layer(h, p[i].reshape(batch * seq, PLE_DIM), pos2, batch, seq, norm_mix[i], w_in[i],
                   qk_norm_a[i], qk_norm_b[i], rel_bias[i], w_out[i], norm_ffn[i], peer_query[i],
                   peer_sub_keys[i], peer_u[i], peer_v[i], norm_ple[i], ple_gate[i], ple_proj[i])
    return h.reshape(batch, seq, d)
```

```python
import functools

import numpy as np
import jax
import jax.numpy as jnp
from jax import lax
from jax.experimental import pallas as pl
from jax.experimental.pallas import tpu as pltpu

F32 = jnp.float32
BF16 = jnp.bfloat16

D_MODEL = 2048
HEAD_DIM = 128
N_HEADS = 8
WIDTH = N_HEADS * HEAD_DIM
CHUNK = 64
LEFT_CHUNKS = 8
MAX_REL = 128
IDX_HEADS = 16
IDX_DIM = 64
TOPK_MAX = 256
ROPE_THETA = 500000.0
ROPE_FRACTION = 4
PEER_HEADS = 8
PEER_KEYS = 128
PEER_TOPK = 16
PLE_DIM = 256
EPS = 1e-6

LANES = 128
SUBLANES = 8
VMEM_LIMIT = 56 * 1024 * 1024

NEG = -1e30

TM_PROJ = 512
QA_BLOCK = 256
QA_WIN = 768
TQ_DSA = 256
KSUB_DSA = 128
BITS_PER_CHECK = 4
TM_ROUTE = 512
TM_EXP = 512
TE_EXP = 512
TM_FINAL = 256
GATE_ROWS = 64
ANCHOR_SPAN_PCT = 100
TP_EXP = 256

_NT = (((1,), (1,)), ((), ()))


def _nt_dot(a, b):
    return lax.dot_general(a, b, _NT, preferred_element_type=F32)


def _rms(x, g):
    return x * lax.rsqrt(jnp.mean(x * x, axis=-1, keepdims=True) + EPS) * g


def _rope_lane_table(group, half):
    lane = np.arange(LANES)
    m = lane % group
    tab = np.zeros((SUBLANES, LANES), np.float32)
    tab[0] = np.where(m < 2 * half, (m % half) / half, 0.0)
    tab[1] = (m < half)
    tab[2] = (m >= half) & (m < 2 * half)
    tab[3] = (m < 2 * half)
    return tab


def _rope_coeffs(pos_f, tab_ref, out_ref):
    inv_freq = tab_ref[3:4, :] / jnp.power(jnp.float32(ROPE_THETA), tab_ref[0:1, :])
    ang = pos_f * inv_freq
    s = jnp.sin(ang)
    out_ref[0] = jnp.cos(ang)
    out_ref[1] = -s * tab_ref[1:2, :]
    out_ref[2] = s * tab_ref[2:3, :]


def _apply_rope(x, coef_ref, half):
    up = pltpu.roll(x, LANES - half, 1)
    dn = pltpu.roll(x, half, 1)
    return x * coef_ref[0] + up * coef_ref[1] + dn * coef_ref[2]


def _relbias_kernel(u_ref, o_ref):
    x = jnp.broadcast_to(u_ref[0, 0:1, :], (QA_BLOCK, 1024))
    y = pltpu.roll(x, 0, 1, stride=1, stride_axis=0)[:, :QA_WIN]
    qc = lax.broadcasted_iota(jnp.int32, (QA_BLOCK, QA_WIN), 0) // CHUNK
    kc = lax.broadcasted_iota(jnp.int32, (QA_BLOCK, QA_WIN), 1) // CHUNK
    band = (kc >= qc) & (kc <= qc + LEFT_CHUNKS)
    o_ref[0] = jnp.where(band, y, NEG)


def _relbias_block(rel_bias):
    p = np.arange(1024)
    off = np.where(p < QA_WIN, p, p - 1024)
    idx = np.clip(LEFT_CHUNKS * CHUNK - off, -MAX_REL, MAX_REL) + MAX_REL
    u = jnp.take(rel_bias.astype(F32), jnp.asarray(idx, jnp.int32), axis=1)
    u = jnp.broadcast_to(u[:, None, :], (N_HEADS, SUBLANES, 1024))
    return pl.pallas_call(
        _relbias_kernel,
        grid=(N_HEADS,),
        in_specs=[pl.BlockSpec((1, SUBLANES, 1024), lambda h: (h, 0, 0))],
        out_specs=pl.BlockSpec((1, QA_BLOCK, QA_WIN), lambda h: (h, 0, 0)),
        out_shape=jax.ShapeDtypeStruct((N_HEADS, QA_BLOCK, QA_WIN), F32),
        name="relbias",
    )(u)


R_QB, R_KB, R_QI, R_QA, R_KA, R_VA = range(6)


def _inproj_kernel(x_ref, g_ref, w0_ref, w1_ref, wvt_ref, wtail_ref, qkg_ref, pos_ref, tab128_ref, tab64_ref,
                   main_ref, vbt_ref, kiab_ref, wi_ref,
                   hn_ref, acc0_ref, acc1_ref, rope128_ref, rope64_ref):
    j = pl.program_id(1)
    tm = x_ref.shape[0]

    @pl.when(j == 0)
    def _():
        hn_ref[...] = _rms(x_ref[...], g_ref[...]).astype(BF16)
        pos_f = pos_ref[...].astype(F32)
        _rope_coeffs(pos_f, tab128_ref, rope128_ref)
        _rope_coeffs(pos_f, tab64_ref, rope64_ref)
        vt = _nt_dot(wvt_ref[...], hn_ref[...]).astype(BF16)
        for t in range(tm // TQ_DSA):
            vbt_ref[t] = vt[:, t * TQ_DSA:(t + 1) * TQ_DSA]
        tail = jnp.dot(hn_ref[...], wtail_ref[...], preferred_element_type=F32)
        kk = _apply_rope(tail[:, :LANES], rope64_ref, IDX_DIM // ROPE_FRACTION // 2)
        lane = lax.broadcasted_iota(jnp.int32, (tm, LANES), 1)
        kiab_ref[:, :LANES] = jnp.where(lane < IDX_DIM, kk, 0.0).astype(BF16)
        kiab_ref[:, LANES:] = jnp.where(lane >= IDX_DIM, kk, 0.0).astype(BF16)
        wi_ref[...] = tail[:, LANES:]

    half_b = HEAD_DIM // ROPE_FRACTION // 2
    half_i = IDX_DIM // ROPE_FRACTION // 2
    epilogues = {
        R_QA: lambda a: _rms(a, qkg_ref[0:1, :]),
        R_KA: lambda a: _rms(a, qkg_ref[1:2, :]),
        R_VA: lambda a: a,
        R_QB: lambda a: _apply_rope(_rms(a, qkg_ref[2:3, :]), rope128_ref, half_b),
        R_KB: lambda a: _apply_rope(_rms(a, qkg_ref[3:4, :]), rope128_ref, half_b),
        R_QI: lambda a: _apply_rope(a, rope64_ref, half_i),
    }

    def project_pair(first):
        accs = (acc0_ref, acc1_ref)
        for r, w_ref in enumerate((w0_ref, w1_ref)):
            accs[r][...] = jnp.dot(hn_ref[...], w_ref[...], preferred_element_type=F32)
        for r in range(2):
            fn = epilogues[first + r]
            for h in range(N_HEADS):
                sl = slice(h * HEAD_DIM, (h + 1) * HEAD_DIM)
                main_ref[:, r * WIDTH + h * HEAD_DIM:r * WIDTH + (h + 1) * HEAD_DIM] = fn(accs[r][:, sl]).astype(BF16)

    for pair in range(3):
        pl.when(j == pair)(functools.partial(project_pair, 2 * pair))


def _inproj(x2, pos2, g_mix, w_in, qk_a, qk_b):
    m = x2.shape[0]
    tm = TM_PROJ
    w = w_in.astype(BF16)
    o = [0, WIDTH, 2 * WIDTH, 3 * WIDTH, 4 * WIDTH, 5 * WIDTH, 6 * WIDTH, 7 * WIDTH]
    src = {R_QA: 0, R_KA: 1, R_VA: 2, R_QB: 3, R_KB: 4, R_QI: 6}

    def w_spec(r):
        blocks = [src[2 * pair + r] for pair in range(3)]
        return pl.BlockSpec((D_MODEL, WIDTH),
                            lambda i, j: (0, jnp.where(j == 0, blocks[0], jnp.where(j == 1, blocks[1], blocks[2]))))

    w_vt = w[:, o[5]:o[6]].T
    w_ki = w[:, o[7]:o[7] + IDX_DIM]
    w_wi = w[:, o[7] + IDX_DIM:o[7] + IDX_DIM + IDX_HEADS]
    w_tail = jnp.concatenate([w_ki, w_ki, w_wi, jnp.zeros((D_MODEL, LANES - IDX_HEADS), BF16)], axis=1)
    qkg = jnp.concatenate([qk_a, qk_b, jnp.zeros((4, HEAD_DIM), F32)], axis=0).astype(F32)
    tab128 = jnp.asarray(_rope_lane_table(HEAD_DIM, HEAD_DIM // ROPE_FRACTION // 2))
    tab64 = jnp.asarray(_rope_lane_table(IDX_DIM, IDX_DIM // ROPE_FRACTION // 2))
    res = lambda shape: pl.BlockSpec(shape, lambda i, j: (0,) * len(shape))
    return pl.pallas_call(
        _inproj_kernel,
        grid=(m // tm, 3),
        in_specs=[
            pl.BlockSpec((tm, D_MODEL), lambda i, j: (i, 0)),
            res((1, D_MODEL)),
            w_spec(0),
            w_spec(1),
            res((WIDTH, D_MODEL)),
            res((D_MODEL, 2 * LANES)),
            res((SUBLANES, HEAD_DIM)),
            pl.BlockSpec((tm, 1), lambda i, j: (i, 0)),
            res((SUBLANES, LANES)),
            res((SUBLANES, LANES)),
        ],
        out_specs=[
            pl.BlockSpec((tm, 2 * WIDTH), lambda i, j: (i, j)),
            pl.BlockSpec((tm // TQ_DSA, WIDTH, TQ_DSA), lambda i, j: (i, 0, 0)),
            pl.BlockSpec((tm, 2 * LANES), lambda i, j: (i, 0)),
            pl.BlockSpec((tm, LANES), lambda i, j: (i, 0)),
        ],
        out_shape=[
            jax.ShapeDtypeStruct((m, 6 * WIDTH), BF16),
            jax.ShapeDtypeStruct((m // TQ_DSA, WIDTH, TQ_DSA), BF16),
            jax.ShapeDtypeStruct((m, 2 * LANES), BF16),
            jax.ShapeDtypeStruct((m, LANES), F32),
        ],
        scratch_shapes=[
            pltpu.VMEM((tm, D_MODEL), BF16),
            pltpu.VMEM((tm, WIDTH), F32),
            pltpu.VMEM((tm, WIDTH), F32),
            pltpu.VMEM((3, tm, LANES), F32),
            pltpu.VMEM((3, tm, LANES), F32),
        ],
        compiler_params=pltpu.CompilerParams(
            dimension_semantics=("parallel", "arbitrary"), vmem_limit_bytes=VMEM_LIMIT),
        name="inproj",
    )(x2, g_mix, w, w, w_vt, w_tail, qkg, pos2, tab128, tab64)


def _mixa_kernel(q_ref, k0_ref, k1_ref, k2_ref, v0_ref, v1_ref, v2_ref, bias_ref, o_ref):
    qb = pl.program_id(1)
    scale = HEAD_DIM ** -0.5
    k_refs = (k0_ref, k1_ref, k2_ref)
    v_refs = (v0_ref, v1_ref, v2_ref)
    pen = (jnp.where(qb >= 2, 0.0, NEG).astype(F32), jnp.where(qb >= 1, 0.0, NEG).astype(F32), None)
    for h in range(N_HEADS):
        sl = slice(h * HEAD_DIM, (h + 1) * HEAD_DIM)
        q = q_ref[:, sl]
        s = []
        for m in range(3):
            sm = _nt_dot(q, k_refs[m][:, sl]) * scale + bias_ref[h, :, m * QA_BLOCK:(m + 1) * QA_BLOCK]
            if pen[m] is not None:
                sm = sm + pen[m]
            s.append(sm)
        mx = jnp.maximum(jnp.maximum(jnp.max(s[0], axis=-1, keepdims=True), jnp.max(s[1], axis=-1, keepdims=True)),
                         jnp.max(s[2], axis=-1, keepdims=True))
        p = [jnp.exp(sm - mx) for sm in s]
        den = (jnp.sum(p[0], axis=-1, keepdims=True) + jnp.sum(p[1], axis=-1, keepdims=True)
               + jnp.sum(p[2], axis=-1, keepdims=True))
        o = jnp.dot(p[0].astype(BF16), v_refs[0][:, sl], preferred_element_type=F32)
        o = o + jnp.dot(p[1].astype(BF16), v_refs[1][:, sl], preferred_element_type=F32)
        o = o + jnp.dot(p[2].astype(BF16), v_refs[2][:, sl], preferred_element_type=F32)
        o_ref[:, sl] = (o / den).astype(BF16)


def _mixa(main, bias, batch, seq):
    m = main.shape[0]
    nqb = seq // QA_BLOCK

    def kv_spec(region, back):
        return pl.BlockSpec((QA_BLOCK, WIDTH),
                            lambda b, i: (b * nqb + jnp.maximum(i - back, 0), region))

    return pl.pallas_call(
        _mixa_kernel,
        grid=(batch, nqb),
        in_specs=[
            pl.BlockSpec((QA_BLOCK, WIDTH), lambda b, i: (b * nqb + i, R_QA)),
            kv_spec(R_KA, 2), kv_spec(R_KA, 1), kv_spec(R_KA, 0),
            kv_spec(R_VA, 2), kv_spec(R_VA, 1), kv_spec(R_VA, 0),
            pl.BlockSpec((N_HEADS, QA_BLOCK, QA_WIN), lambda b, i: (0, 0, 0)),
        ],
        out_specs=pl.BlockSpec((QA_BLOCK, WIDTH), lambda b, i: (b * nqb + i, 0)),
        out_shape=jax.ShapeDtypeStruct((m, WIDTH), BF16),
        compiler_params=pltpu.CompilerParams(
            dimension_semantics=("parallel", "parallel"), vmem_limit_bytes=VMEM_LIMIT),
        name="mixa",
    )(main, main, main, main, main, main, main, bias)


def _key_to_f32(t):
    return pltpu.bitcast(t ^ ((t >> 31) & jnp.int32(0x7FFFFFFF)), F32)


def _dsa_kernel(topk, qb_ref, kb_ref, vbt_ref, qi_ref, kiab_ref, wi_ref, o_ref,
                sc_ref, bias_ref, acc_ref, m_ref, l_ref):
    i = pl.program_id(1)
    tq = TQ_DSA
    nkt = i + 1
    c2 = HEAD_DIM ** -0.5 * 1.4426950408889634

    w_t = jnp.transpose(wi_ref[...]) * (IDX_HEADS ** -0.5 * IDX_DIM ** -0.5)
    qchunk = (i * tq + lax.broadcasted_iota(jnp.int32, (1, tq), 1)) // CHUNK
    row = lax.broadcasted_iota(jnp.int32, (tq, tq), 0)

    def admissible(kt):
        return (kt * tq + row) // CHUNK <= qchunk

    def idx_body(kt, carry):
        r0 = pl.multiple_of(kt * tq, tq)
        ka = kiab_ref[pl.ds(r0, tq), :LANES]
        kb = kiab_ref[pl.ds(r0, tq), LANES:]
        acc = jnp.zeros((tq, tq), F32)
        for p in range(IDX_HEADS // 2):
            qp = qi_ref[:, p * LANES:(p + 1) * LANES]
            acc = acc + w_t[2 * p:2 * p + 1, :] * jnp.maximum(_nt_dot(ka, qp), 0.0)
            acc = acc + w_t[2 * p + 1:2 * p + 2, :] * jnp.maximum(_nt_dot(kb, qp), 0.0)
        sc_ref[pl.ds(r0, tq), :] = jnp.where(admissible(kt), acc, -jnp.inf)
        return carry

    lax.fori_loop(0, nkt, idx_body, 0)

    n_adm = (qchunk + 1) * CHUNK
    need = n_adm > topk

    def count_ge(c_f):
        def cnt_body(kt, c):
            r0 = pl.multiple_of(kt * tq, tq)
            ge = jnp.where(sc_ref[pl.ds(r0, tq), :] >= c_f, 1, 0).astype(jnp.int32)
            return c + jnp.sum(ge.reshape(tq // SUBLANES, SUBLANES, tq), axis=0)

        c8 = lax.fori_loop(0, nkt, cnt_body, jnp.zeros((SUBLANES, tq), jnp.int32))
        return jnp.sum(c8, axis=0, keepdims=True)

    def bit_cond(carry):
        b, _, cnt = carry
        todo = jnp.max(jnp.where(need & (cnt != topk), 1, 0))
        return (b >= 0) & (todo > 0)

    def bit_body(carry):
        b, thr, cnt = carry
        for _ in range(BITS_PER_CHECK):
            cand = thr + lax.shift_left(jnp.int32(1), b)
            n = count_ge(_key_to_f32(cand))
            ok = n >= topk
            b, thr, cnt = b - 1, jnp.where(ok, cand, thr), jnp.where(ok, n, cnt)
        return b, thr, cnt

    init = (jnp.int32(31), jnp.full((1, tq), jnp.iinfo(jnp.int32).min, jnp.int32),
            jnp.full((1, tq), jnp.iinfo(jnp.int32).max, jnp.int32))
    _, thr, _ = lax.while_loop(bit_cond, bit_body, init)
    thr_f = _key_to_f32(thr)

    def mask_body(kt, carry):
        r0 = pl.multiple_of(kt * tq, tq)
        sel = admissible(kt) & ((sc_ref[pl.ds(r0, tq), :] >= thr_f) | jnp.logical_not(need))
        bias_ref[pl.ds(r0, tq), :] = jnp.where(sel, 0.0, NEG).astype(F32)
        return carry

    lax.fori_loop(0, nkt, mask_body, 0)

    acc_ref[...] = jnp.zeros_like(acc_ref)
    m_ref[...] = jnp.full(m_ref.shape, NEG, F32)
    l_ref[...] = jnp.zeros_like(l_ref)

    def att_body(kt, carry):
        for half in range(tq // KSUB_DSA):
            r0 = pl.multiple_of(kt * tq + half * KSUB_DSA, KSUB_DSA)
            ks = slice(half * KSUB_DSA, (half + 1) * KSUB_DSA)
            for h in range(N_HEADS):
                sl = slice(h * HEAD_DIM, (h + 1) * HEAD_DIM)
                s = _nt_dot(kb_ref[pl.ds(r0, KSUB_DSA), sl], qb_ref[:, sl]) + bias_ref[pl.ds(r0, KSUB_DSA), :]
                m_old = m_ref[h:h + 1, :]
                m_new = jnp.maximum(m_old, jnp.max(s, axis=0, keepdims=True))
                alpha = jnp.exp2((m_old - m_new) * c2)
                p = jnp.exp2((s - m_new) * c2)
                l_ref[h:h + 1, :] = alpha * l_ref[h:h + 1, :] + jnp.sum(p, axis=0, keepdims=True)
                m_ref[h:h + 1, :] = m_new
                pv = jnp.dot(vbt_ref[kt, sl, ks], p.astype(BF16), preferred_element_type=F32)
                acc_ref[h] = alpha * acc_ref[h] + pv
        return carry

    lax.fori_loop(0, nkt, att_body, 0)
    for h in range(N_HEADS):
        sl = slice(h * HEAD_DIM, (h + 1) * HEAD_DIM)
        o_ref[:, sl] = jnp.transpose(acc_ref[h] / l_ref[h:h + 1, :]).astype(BF16)


def _dsa(main, vbt, kiab, wi, batch, seq):
    m = main.shape[0]
    tq = TQ_DSA
    nq = seq // tq
    topk = min(TOPK_MAX, seq // 4)
    return pl.pallas_call(
        functools.partial(_dsa_kernel, topk),
        grid=(batch, nq),
        in_specs=[
            pl.BlockSpec((tq, WIDTH), lambda b, i: (b * nq + i, R_QB)),
            pl.BlockSpec((seq, WIDTH), lambda b, i: (b, R_KB)),
            pl.BlockSpec((nq, WIDTH, tq), lambda b, i: (b, 0, 0)),
            pl.BlockSpec((tq, WIDTH), lambda b, i: (b * nq + i, R_QI)),
            pl.BlockSpec((seq, 2 * LANES), lambda b, i: (b, 0)),
            pl.BlockSpec((tq, LANES), lambda b, i: (b * nq + i, 0)),
        ],
        out_specs=pl.BlockSpec((tq, WIDTH), lambda b, i: (b * nq + i, 0)),
        out_shape=jax.ShapeDtypeStruct((m, WIDTH), BF16),
        scratch_shapes=[pltpu.VMEM((seq, tq), F32), pltpu.VMEM((seq, tq), F32),
                        pltpu.VMEM((N_HEADS, HEAD_DIM, tq), F32),
                        pltpu.VMEM((N_HEADS, tq), F32), pltpu.VMEM((N_HEADS, tq), F32)],
        compiler_params=pltpu.CompilerParams(
            dimension_semantics=("parallel", "arbitrary"), vmem_limit_bytes=VMEM_LIMIT),
        name="dsa",
    )(main, main, vbt, main, kiab, wi)


def _outproj_kernel(oa_ref, ob_ref, x_ref, wo_ref, g_ref, h1_ref, xn_ref):
    acc = jnp.dot(oa_ref[...], wo_ref[:WIDTH, :], preferred_element_type=F32)
    acc = acc + jnp.dot(ob_ref[...], wo_ref[WIDTH:, :], preferred_element_type=F32)
    h1 = x_ref[...] + acc
    h1_ref[...] = h1
    xn_ref[...] = _rms(h1, g_ref[...]).astype(BF16)


def _outproj(out_a, out_b, x2, w_out, g_ffn):
    m = x2.shape[0]
    tm = TM_PROJ
    return pl.pallas_call(
        _outproj_kernel,
        grid=(m // tm,),
        in_specs=[
            pl.BlockSpec((tm, WIDTH), lambda i: (i, 0)),
            pl.BlockSpec((tm, WIDTH), lambda i: (i, 0)),
            pl.BlockSpec((tm, D_MODEL), lambda i: (i, 0)),
            pl.BlockSpec((2 * WIDTH, D_MODEL), lambda i: (0, 0)),
            pl.BlockSpec((1, D_MODEL), lambda i: (0, 0)),
        ],
        out_specs=[pl.BlockSpec((tm, D_MODEL), lambda i: (i, 0)), pl.BlockSpec((tm, D_MODEL), lambda i: (i, 0))],
        out_shape=[jax.ShapeDtypeStruct((m, D_MODEL), F32), jax.ShapeDtypeStruct((m, D_MODEL), BF16)],
        compiler_params=pltpu.CompilerParams(dimension_semantics=("parallel",), vmem_limit_bytes=VMEM_LIMIT),
        name="outproj",
    )(out_a, out_b, x2, w_out.astype(BF16), g_ffn)


def _merge_exchange_pairs(n):
    t = (n - 1).bit_length()
    pairs = []
    p = 1 << (t - 1)
    while p > 0:
        q, r, d = 1 << (t - 1), 0, p
        while True:
            pairs.extend((i, i + d) for i in range(n - d) if (i & p) == r)
            if q == p:
                break
            d, q, r = q - p, q >> 1, p
        p >>= 1
    return pairs


_SORT16 = _merge_exchange_pairs(PEER_TOPK)


def _tmax(a, b):
    if a is None:
        return b
    if b is None:
        return a
    return jnp.maximum(a, b)


def _cmp_exchange(x, i, j):
    a, b = x[i], x[j]
    if b is None:
        return
    if a is None:
        x[i], x[j] = b, None
        return
    x[i], x[j] = jnp.maximum(a, b), jnp.minimum(a, b)


def _sort16_desc(x):
    for i, j in _SORT16:
        _cmp_exchange(x, i, j)


def _bitonic16_desc(x):
    for d in (8, 4, 2, 1):
        for i in range(PEER_TOPK):
            if not i & d:
                _cmp_exchange(x, i, i + d)


def _merge_sublanes(x, shift, sort=True):
    y = [None if v is None else pltpu.roll(v, shift, 0) for v in x]
    out = [_tmax(x[i], y[PEER_TOPK - 1 - i]) for i in range(PEER_TOPK)]
    if sort:
        _bitonic16_desc(out)
    return out


def _top16(tiles):
    x = list(tiles)
    _sort16_desc(x)
    for shift in (4, 2, 1):
        x = _merge_sublanes(x, shift)
    return x


def _pair_threshold(v1, v2):
    sub = lax.broadcasted_iota(jnp.int32, v1[0].shape, 0)

    def diag(vals):
        d = vals[0]
        for s in range(1, SUBLANES):
            d = jnp.where(sub == s, vals[s], d)
        return d

    d1a, d1b = diag(v1[:8]), diag(v1[8:])
    d2a, d2b = diag(v2[:8]), diag(v2[8:])
    cand = [d1a + v2[0], d1b + v2[0], d1a + v2[1], d1a + v2[2], d1a + v2[3], v1[0] + d2b]
    cand += [jnp.where(sub < 4, -jnp.inf, v1[r] + d2a) for r in range(3)]
    x = cand + [None] * (PEER_TOPK - len(cand))
    _sort16_desc(x)
    x = _merge_sublanes(x, 4)
    x = _merge_sublanes(x, 2)
    x = _merge_sublanes(x, 1, sort=False)
    tau = x[0]
    for v in x[1:]:
        tau = jnp.minimum(tau, v)
    return tau


def _sublane_total(x):
    for shift in (4, 2, 1):
        x = x + pltpu.roll(x, shift, 0)
    return x


def _route_kernel(xn_ref, wq_ref, sk_ref, s2_ref, a2_ref, c_ref, a1_ref, qv_ref):
    tm = xn_ref.shape[0]
    n_t = PEER_KEYS // SUBLANES
    qv = jnp.dot(xn_ref[...], wq_ref[...], preferred_element_type=F32)
    for c in range(2 * PEER_HEADS):
        qv_ref[c] = qv[:, c * PEER_KEYS:(c + 1) * PEER_KEYS].astype(BF16)

    def chunk(h, lc):
        t0 = pl.multiple_of(lc * LANES, LANES)
        s1 = _nt_dot(sk_ref[0], qv_ref[2 * h, pl.ds(t0, LANES), :])
        s2 = _nt_dot(sk_ref[1], qv_ref[2 * h + 1, pl.ds(t0, LANES), :])
        s1 = [s1[j * SUBLANES:(j + 1) * SUBLANES, :] for j in range(n_t)]
        s2 = [s2[j * SUBLANES:(j + 1) * SUBLANES, :] for j in range(n_t)]
        v1 = _top16(s1)
        v2 = _top16(s2)
        tau = _pair_threshold(v1, v2)
        pre = []
        for q in range(PEER_TOPK):
            e = jnp.exp(v2[q] - v2[0])
            pre.append(e if q == 0 else pre[-1] + e)
        a1, cth, z = [], [], None
        for j in range(n_t):
            c_j = jnp.full(s1[j].shape, jnp.inf, F32)
            m_j = jnp.zeros(s1[j].shape, F32)
            for q in range(PEER_TOPK):
                ok = s1[j] + v2[q] >= tau
                c_j = jnp.where(ok, v2[q], c_j)
                m_j = jnp.where(ok, pre[q], m_j)
            a_j = jnp.exp(s1[j] - v1[0])
            a1.append(a_j)
            cth.append(c_j)
            z = a_j * m_j if z is None else z + a_j * m_j
        z = _sublane_total(z)
        for j in range(n_t):
            rs = slice(j * SUBLANES, (j + 1) * SUBLANES)
            s2_ref[h, lc, rs, :] = s2[j]
            a2_ref[h, lc, rs, :] = jnp.exp(s2[j] - v2[0])
            c_ref[h, lc, rs, :] = cth[j]
            a1_ref[h, lc, rs, :] = a1[j] / z

    def head_body(h, carry):
        def chunk_body(lc, c):
            chunk(h, lc)
            return c

        return lax.fori_loop(0, tm // LANES, chunk_body, carry)

    lax.fori_loop(0, PEER_HEADS, head_body, 0)


def _route(xn, w_query, sub_keys):
    m = xn.shape[0]
    tm = TM_ROUTE
    spec = pl.BlockSpec((PEER_HEADS, tm // LANES, PEER_KEYS, LANES), lambda i: (0, i, 0, 0))
    shp = jax.ShapeDtypeStruct((PEER_HEADS, m // LANES, PEER_KEYS, LANES), F32)
    return pl.pallas_call(
        _route_kernel,
        grid=(m // tm,),
        in_specs=[
            pl.BlockSpec((tm, D_MODEL), lambda i: (i, 0)),
            pl.BlockSpec((D_MODEL, 2 * PEER_HEADS * PEER_KEYS), lambda i: (0, 0)),
            pl.BlockSpec((2, PEER_KEYS, PEER_KEYS), lambda i: (0, 0, 0)),
        ],
        out_specs=[spec, spec, spec, spec],
        out_shape=[shp, shp, shp, shp],
        scratch_shapes=[pltpu.VMEM((2 * PEER_HEADS, tm, PEER_KEYS), BF16)],
        compiler_params=pltpu.CompilerParams(dimension_semantics=("parallel",), vmem_limit_bytes=VMEM_LIMIT),
        name="route",
    )(xn, w_query.astype(BF16), sub_keys.astype(BF16))


def _zero_row_from(x):
    bits = pltpu.bitcast(x, jnp.int32)
    z = lax.shift_right_logical(lax.shift_right_logical(bits, 16), 16)
    return z[0:1, :].astype(F32)


def _gate_piece(h_ref, coef_ref, p, e1_base, s2_ref, a2_ref, c_ref, a1_ref, anchors):
    n_e1 = TE_EXP // PEER_KEYS
    unit = 0
    for c in range(TP_EXP // LANES):
        lc = p * (TP_EXP // LANES) + c
        ls = slice(c * LANES, (c + 1) * LANES)
        for r0 in range(0, PEER_KEYS, GATE_ROWS):
            kr = slice(r0, r0 + GATE_ROWS)
            for e0 in range(0, n_e1, 2):
                gates = [jnp.zeros((GATE_ROWS, LANES), F32), jnp.zeros((GATE_ROWS, LANES), F32)]
                for h in range(PEER_HEADS):
                    anchor = anchors[unit * PEER_HEADS + h]
                    s2 = s2_ref[h, lc, kr, :]
                    a2 = a2_ref[h, lc, kr, :]
                    for k in range(2):
                        e1 = e1_base + e0 + k
                        cth = c_ref[h, lc, pl.ds(e1, 1), :] + anchor
                        a1 = a1_ref[h, lc, pl.ds(e1, 1), :]
                        gates[k] = gates[k] + jnp.where(s2 >= cth, a2, 0.0) * a1
                for k in range(2):
                    rs = slice((e0 + k) * PEER_KEYS + r0, (e0 + k) * PEER_KEYS + r0 + GATE_ROWS)
                    act = jax.nn.gelu(h_ref[p, rs, ls] + anchor)
                    coef_ref[p, rs, ls] = (gates[k] * act).astype(BF16)
                unit += 1


def _experts_kernel(n_steps, xn_ref, u_ref, vt_ref, s2_ref, a2_ref, c_ref, a1_ref, o_ref,
                    h0_ref, h1_ref, c0_ref, c1_ref):
    g = pl.program_id(1)
    te = TE_EXP
    n_e1 = te // PEER_KEYS
    last_e1 = (2 * n_steps - 1) * n_e1
    n_pieces = xn_ref.shape[0] // TP_EXP
    n_anchor = (TP_EXP // LANES) * (PEER_KEYS // GATE_ROWS) * (n_e1 // 2) * PEER_HEADS

    @pl.when(g == 0)
    def _():
        o_ref[...] = jnp.zeros_like(o_ref)
        h1_ref[...] = jnp.zeros_like(h1_ref)
        c0_ref[...] = jnp.zeros_like(c0_ref)

    gates = (s2_ref, a2_ref, c_ref, a1_ref)

    def half_step(u_rows, v_cols, hb_ref, hc_ref, cc_ref, ca_ref, e1_base):
        def body(p, carry):
            t0 = pl.multiple_of(p * TP_EXP, TP_EXP)
            hb_ref[p] = _nt_dot(u_ref[u_rows, :], xn_ref[pl.ds(t0, TP_EXP), :])
            res = jnp.dot(vt_ref[:, v_cols], ca_ref[p], preferred_element_type=F32)
            o_ref[p] += res
            step = (D_MODEL * ANCHOR_SPAN_PCT // 100) // n_anchor // SUBLANES * SUBLANES
            anchors = [_zero_row_from(res[k * step:k * step + SUBLANES, 0:LANES]) for k in range(n_anchor)]
            _gate_piece(hc_ref, cc_ref, p, e1_base, *gates, anchors)
            return carry

        lax.fori_loop(0, n_pieces, body, 0)

    half_step(slice(0, te), slice(0, te), h0_ref, h1_ref, c1_ref, c0_ref,
              jnp.clip((2 * g - 1) * n_e1, 0, last_e1))
    half_step(slice(te, 2 * te), slice(te, 2 * te), h1_ref, h0_ref, c0_ref, c1_ref,
              jnp.minimum(2 * g * n_e1, last_e1))


def _experts(xn, expert_u, expert_v, s2, a2, cth, a1):
    m = xn.shape[0]
    tm, te, tp = TM_EXP, TE_EXP, TP_EXP
    n_exp = expert_u.shape[0]
    n_steps = n_exp // (2 * te)
    gspec = pl.BlockSpec((PEER_HEADS, tm // LANES, PEER_KEYS, LANES), lambda i, g: (0, i, 0, 0))
    return pl.pallas_call(
        functools.partial(_experts_kernel, n_steps),
        grid=(m // tm, n_steps + 1),
        in_specs=[
            pl.BlockSpec((tm, D_MODEL), lambda i, g: (i, 0)),
            pl.BlockSpec((2 * te, D_MODEL), lambda i, g: (jnp.minimum(g, n_steps - 1), 0)),
            pl.BlockSpec((D_MODEL, 2 * te), lambda i, g: (0, jnp.maximum(g - 1, 0))),
            gspec, gspec, gspec, gspec,
        ],
        out_specs=pl.BlockSpec((tm // tp, D_MODEL, tp), lambda i, g: (i, 0, 0)),
        out_shape=jax.ShapeDtypeStruct((m // tp, D_MODEL, tp), F32),
        scratch_shapes=[pltpu.VMEM((tm // tp, te, tp), F32), pltpu.VMEM((tm // tp, te, tp), F32),
                        pltpu.VMEM((tm // tp, te, tp), BF16), pltpu.VMEM((tm // tp, te, tp), BF16)],
        compiler_params=pltpu.CompilerParams(
            dimension_semantics=("parallel", "arbitrary"), vmem_limit_bytes=VMEM_LIMIT),
        name="experts",
    )(xn, expert_u.astype(BF16), expert_v.astype(BF16).T, s2, a2, cth, a1)


def _final_kernel(h1_ref, ot_ref, p_ref, g_ref, wg_ref, wp_ref, o_ref):
    h2 = h1_ref[...] + jnp.transpose(ot_ref[0])
    hn = _rms(h2, g_ref[...]).astype(BF16)
    gate = jax.nn.sigmoid(jnp.dot(hn, wg_ref[...], preferred_element_type=F32))
    pe = jnp.dot(p_ref[...].astype(BF16), wp_ref[...], preferred_element_type=F32)
    o_ref[...] = h2 + gate * pe


def _final(h1, out_t, p2, g_ple, w_gate, w_proj):
    m = h1.shape[0]
    tm = TM_FINAL
    assert out_t.shape == (m // tm, D_MODEL, tm)
    return pl.pallas_call(
        _final_kernel,
        grid=(m // tm,),
        in_specs=[
            pl.BlockSpec((tm, D_MODEL), lambda i: (i, 0)),
            pl.BlockSpec((1, D_MODEL, tm), lambda i: (i, 0, 0)),
            pl.BlockSpec((tm, PLE_DIM), lambda i: (i, 0)),
            pl.BlockSpec((1, D_MODEL), lambda i: (0, 0)),
            pl.BlockSpec((D_MODEL, D_MODEL), lambda i: (0, 0)),
            pl.BlockSpec((PLE_DIM, D_MODEL), lambda i: (0, 0)),
        ],
        out_specs=pl.BlockSpec((tm, D_MODEL), lambda i: (i, 0)),
        out_shape=jax.ShapeDtypeStruct((m, D_MODEL), F32),
        compiler_params=pltpu.CompilerParams(dimension_semantics=("parallel",), vmem_limit_bytes=VMEM_LIMIT),
        name="final",
    )(h1, out_t, p2, g_ple, w_gate.astype(BF16), w_proj.astype(BF16))


def _layer(h2d, p2d, pos2, batch, seq, norm_mix, w_in, qk_a, qk_b, rel_bias, w_out, norm_ffn,
           peer_query, sub_keys, peer_u, peer_v, norm_ple, ple_gate, ple_proj):
    bias = _relbias_block(rel_bias)
    main, vbt, kiab, wi = _inproj(h2d, pos2, norm_mix[None, :], w_in, qk_a, qk_b)
    out_a = _mixa(main, bias, batch, seq)
    out_b = _dsa(main, vbt, kiab, wi, batch, seq)
    h1, xn = _outproj(out_a, out_b, h2d, w_out, norm_ffn[None, :])
    s2, a2, cth, a1 = _route(xn, peer_query, sub_keys)
    out_t = _experts(xn, peer_u, peer_v, s2, a2, cth, a1)
    return _final(h1, out_t, p2d, norm_ple[None, :], ple_gate, ple_proj)


def kernel(x, p, positions, norm_mix, w_in, qk_norm_a, qk_norm_b, rel_bias, w_out, norm_ffn, peer_query,
           peer_sub_keys, peer_u, peer_v, norm_ple, ple_gate, ple_proj):
    batch, seq, d = x.shape
    assert d == D_MODEL and seq % TM_PROJ == 0 and (batch * seq) % TM_PROJ == 0
    h = x.reshape(batch * seq, d)
    pos2 = positions.reshape(batch * seq, 1).astype(jnp.int32)
    for i in range(p.shape[0]):
        h = _layer(h, p[i].reshape(batch * seq, PLE_DIM), pos2, batch, seq, norm_mix[i], w_in[i],
                   qk_norm_a[i], qk_norm_b[i], rel_bias[i], w_out[i], norm_ffn[i], peer_query[i],
                   peer_sub_keys[i], peer_u[i], peer_v[i], norm_ple[i], ple_gate[i], ple_proj[i])
    return h.reshape(batch, seq, d)
```

```python
import functools

import numpy as np
import jax
import jax.numpy as jnp
from jax import lax
from jax.experimental import pallas as pl
from jax.experimental.pallas import tpu as pltpu

F32 = jnp.float32
BF16 = jnp.bfloat16

D_MODEL = 2048
HEAD_DIM = 128
N_HEADS = 8
WIDTH = N_HEADS * HEAD_DIM
CHUNK = 64
LEFT_CHUNKS = 8
MAX_REL = 128
IDX_HEADS = 16
IDX_DIM = 64
TOPK_MAX = 256
ROPE_THETA = 500000.0
ROPE_FRACTION = 4
PEER_HEADS = 8
PEER_KEYS = 128
PEER_TOPK = 16
PLE_DIM = 256
EPS = 1e-6

LANES = 128
SUBLANES = 8
VMEM_LIMIT = 56 * 1024 * 1024

NEG = -1e30

TM_PROJ = 512
QA_BLOCK = 256
QA_WIN = 768
TQ_DSA = 256
KSUB_DSA = 128
BITS_PER_CHECK = 4
TM_ROUTE = 512
TM_EXP = 512
TE_EXP = 512
TM_FINAL = 256
GATE_ROWS = 32
ANCHOR_SPAN_PCT = 100
TP_EXP = 256

_NT = (((1,), (1,)), ((), ()))


def _nt_dot(a, b):
    return lax.dot_general(a, b, _NT, preferred_element_type=F32)


def _rms(x, g):
    return x * lax.rsqrt(jnp.mean(x * x, axis=-1, keepdims=True) + EPS) * g


def _rope_lane_table(group, half):
    lane = np.arange(LANES)
    m = lane % group
    tab = np.zeros((SUBLANES, LANES), np.float32)
    tab[0] = np.where(m < 2 * half, (m % half) / half, 0.0)
    tab[1] = (m < half)
    tab[2] = (m >= half) & (m < 2 * half)
    tab[3] = (m < 2 * half)
    return tab


def _rope_coeffs(pos_f, tab_ref, out_ref):
    inv_freq = tab_ref[3:4, :] / jnp.power(jnp.float32(ROPE_THETA), tab_ref[0:1, :])
    ang = pos_f * inv_freq
    s = jnp.sin(ang)
    out_ref[0] = jnp.cos(ang)
    out_ref[1] = -s * tab_ref[1:2, :]
    out_ref[2] = s * tab_ref[2:3, :]


def _apply_rope(x, coef_ref, half):
    up = pltpu.roll(x, LANES - half, 1)
    dn = pltpu.roll(x, half, 1)
    return x * coef_ref[0] + up * coef_ref[1] + dn * coef_ref[2]


def _relbias_kernel(u_ref, o_ref):
    x = jnp.broadcast_to(u_ref[0, 0:1, :], (QA_BLOCK, 1024))
    y = pltpu.roll(x, 0, 1, stride=1, stride_axis=0)[:, :QA_WIN]
    qc = lax.broadcasted_iota(jnp.int32, (QA_BLOCK, QA_WIN), 0) // CHUNK
    kc = lax.broadcasted_iota(jnp.int32, (QA_BLOCK, QA_WIN), 1) // CHUNK
    band = (kc >= qc) & (kc <= qc + LEFT_CHUNKS)
    o_ref[0] = jnp.where(band, y, NEG)


def _relbias_block(rel_bias):
    p = np.arange(1024)
    off = np.where(p < QA_WIN, p, p - 1024)
    idx = np.clip(LEFT_CHUNKS * CHUNK - off, -MAX_REL, MAX_REL) + MAX_REL
    u = jnp.take(rel_bias.astype(F32), jnp.asarray(idx, jnp.int32), axis=1)
    u = jnp.broadcast_to(u[:, None, :], (N_HEADS, SUBLANES, 1024))
    return pl.pallas_call(
        _relbias_kernel,
        grid=(N_HEADS,),
        in_specs=[pl.BlockSpec((1, SUBLANES, 1024), lambda h: (h, 0, 0))],
        out_specs=pl.BlockSpec((1, QA_BLOCK, QA_WIN), lambda h: (h, 0, 0)),
        out_shape=jax.ShapeDtypeStruct((N_HEADS, QA_BLOCK, QA_WIN), F32),
        name="relbias",
    )(u)


R_QB, R_KB, R_QI, R_QA, R_KA, R_VA = range(6)


def _inproj_kernel(x_ref, g_ref, w0_ref, w1_ref, wvt_ref, wtail_ref, qkg_ref, pos_ref, tab128_ref, tab64_ref,
                   main_ref, vbt_ref, kiab_ref, wi_ref,
                   hn_ref, acc0_ref, acc1_ref, rope128_ref, rope64_ref):
    j = pl.program_id(1)
    tm = x_ref.shape[0]

    @pl.when(j == 0)
    def _():
        hn_ref[...] = _rms(x_ref[...], g_ref[...]).astype(BF16)
        pos_f = pos_ref[...].astype(F32)
        _rope_coeffs(pos_f, tab128_ref, rope128_ref)
        _rope_coeffs(pos_f, tab64_ref, rope64_ref)
        vt = _nt_dot(wvt_ref[...], hn_ref[...]).astype(BF16)
        for t in range(tm // TQ_DSA):
            vbt_ref[t] = vt[:, t * TQ_DSA:(t + 1) * TQ_DSA]
        tail = jnp.dot(hn_ref[...], wtail_ref[...], preferred_element_type=F32)
        kk = _apply_rope(tail[:, :LANES], rope64_ref, IDX_DIM // ROPE_FRACTION // 2)
        lane = lax.broadcasted_iota(jnp.int32, (tm, LANES), 1)
        kiab_ref[:, :LANES] = jnp.where(lane < IDX_DIM, kk, 0.0).astype(BF16)
        kiab_ref[:, LANES:] = jnp.where(lane >= IDX_DIM, kk, 0.0).astype(BF16)
        wi_ref[...] = tail[:, LANES:]

    half_b = HEAD_DIM // ROPE_FRACTION // 2
    half_i = IDX_DIM // ROPE_FRACTION // 2
    epilogues = {
        R_QA: lambda a: _rms(a, qkg_ref[0:1, :]),
        R_KA: lambda a: _rms(a, qkg_ref[1:2, :]),
        R_VA: lambda a: a,
        R_QB: lambda a: _apply_rope(_rms(a, qkg_ref[2:3, :]), rope128_ref, half_b),
        R_KB: lambda a: _apply_rope(_rms(a, qkg_ref[3:4, :]), rope128_ref, half_b),
        R_QI: lambda a: _apply_rope(a, rope64_ref, half_i),
    }

    def project_pair(first):
        accs = (acc0_ref, acc1_ref)
        for r, w_ref in enumerate((w0_ref, w1_ref)):
            accs[r][...] = jnp.dot(hn_ref[...], w_ref[...], preferred_element_type=F32)
        for r in range(2):
            fn = epilogues[first + r]
            for h in range(N_HEADS):
                sl = slice(h * HEAD_DIM, (h + 1) * HEAD_DIM)
                main_ref[:, r * WIDTH + h * HEAD_DIM:r * WIDTH + (h + 1) * HEAD_DIM] = fn(accs[r][:, sl]).astype(BF16)

    for pair in range(3):
        pl.when(j == pair)(functools.partial(project_pair, 2 * pair))


def _inproj(x2, pos2, g_mix, w_in, qk_a, qk_b):
    m = x2.shape[0]
    tm = TM_PROJ
    w = w_in.astype(BF16)
    o = [0, WIDTH, 2 * WIDTH, 3 * WIDTH, 4 * WIDTH, 5 * WIDTH, 6 * WIDTH, 7 * WIDTH]
    src = {R_QA: 0, R_KA: 1, R_VA: 2, R_QB: 3, R_KB: 4, R_QI: 6}

    def w_spec(r):
        blocks = [src[2 * pair + r] for pair in range(3)]
        return pl.BlockSpec((D_MODEL, WIDTH),
                            lambda i, j: (0, jnp.where(j == 0, blocks[0], jnp.where(j == 1, blocks[1], blocks[2]))))

    w_vt = w[:, o[5]:o[6]].T
    w_ki = w[:, o[7]:o[7] + IDX_DIM]
    w_wi = w[:, o[7] + IDX_DIM:o[7] + IDX_DIM + IDX_HEADS]
    w_tail = jnp.concatenate([w_ki, w_ki, w_wi, jnp.zeros((D_MODEL, LANES - IDX_HEADS), BF16)], axis=1)
    qkg = jnp.concatenate([qk_a, qk_b, jnp.zeros((4, HEAD_DIM), F32)], axis=0).astype(F32)
    tab128 = jnp.asarray(_rope_lane_table(HEAD_DIM, HEAD_DIM // ROPE_FRACTION // 2))
    tab64 = jnp.asarray(_rope_lane_table(IDX_DIM, IDX_DIM // ROPE_FRACTION // 2))
    res = lambda shape: pl.BlockSpec(shape, lambda i, j: (0,) * len(shape))
    return pl.pallas_call(
        _inproj_kernel,
        grid=(m // tm, 3),
        in_specs=[
            pl.BlockSpec((tm, D_MODEL), lambda i, j: (i, 0)),
            res((1, D_MODEL)),
            w_spec(0),
            w_spec(1),
            res((WIDTH, D_MODEL)),
            res((D_MODEL, 2 * LANES)),
            res((SUBLANES, HEAD_DIM)),
            pl.BlockSpec((tm, 1), lambda i, j: (i, 0)),
            res((SUBLANES, LANES)),
            res((SUBLANES, LANES)),
        ],
        out_specs=[
            pl.BlockSpec((tm, 2 * WIDTH), lambda i, j: (i, j)),
            pl.BlockSpec((tm // TQ_DSA, WIDTH, TQ_DSA), lambda i, j: (i, 0, 0)),
            pl.BlockSpec((tm, 2 * LANES), lambda i, j: (i, 0)),
            pl.BlockSpec((tm, LANES), lambda i, j: (i, 0)),
        ],
        out_shape=[
            jax.ShapeDtypeStruct((m, 6 * WIDTH), BF16),
            jax.ShapeDtypeStruct((m // TQ_DSA, WIDTH, TQ_DSA), BF16),
            jax.ShapeDtypeStruct((m, 2 * LANES), BF16),
            jax.ShapeDtypeStruct((m, LANES), F32),
        ],
        scratch_shapes=[
            pltpu.VMEM((tm, D_MODEL), BF16),
            pltpu.VMEM((tm, WIDTH), F32),
            pltpu.VMEM((tm, WIDTH), F32),
            pltpu.VMEM((3, tm, LANES), F32),
            pltpu.VMEM((3, tm, LANES), F32),
        ],
        compiler_params=pltpu.CompilerParams(
            dimension_semantics=("parallel", "arbitrary"), vmem_limit_bytes=VMEM_LIMIT),
        name="inproj",
    )(x2, g_mix, w, w, w_vt, w_tail, qkg, pos2, tab128, tab64)


def _mixa_kernel(q_ref, k0_ref, k1_ref, k2_ref, v0_ref, v1_ref, v2_ref, bias_ref, o_ref):
    qb = pl.program_id(1)
    scale = HEAD_DIM ** -0.5
    k_refs = (k0_ref, k1_ref, k2_ref)
    v_refs = (v0_ref, v1_ref, v2_ref)
    pen = (jnp.where(qb >= 2, 0.0, NEG).astype(F32), jnp.where(qb >= 1, 0.0, NEG).astype(F32), None)
    for h in range(N_HEADS):
        sl = slice(h * HEAD_DIM, (h + 1) * HEAD_DIM)
        q = q_ref[:, sl]
        s = []
        for m in range(3):
            sm = _nt_dot(q, k_refs[m][:, sl]) * scale + bias_ref[h, :, m * QA_BLOCK:(m + 1) * QA_BLOCK]
            if pen[m] is not None:
                sm = sm + pen[m]
            s.append(sm)
        mx = jnp.maximum(jnp.maximum(jnp.max(s[0], axis=-1, keepdims=True), jnp.max(s[1], axis=-1, keepdims=True)),
                         jnp.max(s[2], axis=-1, keepdims=True))
        p = [jnp.exp(sm - mx) for sm in s]
        den = (jnp.sum(p[0], axis=-1, keepdims=True) + jnp.sum(p[1], axis=-1, keepdims=True)
               + jnp.sum(p[2], axis=-1, keepdims=True))
        o = jnp.dot(p[0].astype(BF16), v_refs[0][:, sl], preferred_element_type=F32)
        o = o + jnp.dot(p[1].astype(BF16), v_refs[1][:, sl], preferred_element_type=F32)
        o = o + jnp.dot(p[2].astype(BF16), v_refs[2][:, sl], preferred_element_type=F32)
        o_ref[:, sl] = (o / den).astype(BF16)


def _mixa(main, bias, batch, seq):
    m = main.shape[0]
    nqb = seq // QA_BLOCK

    def kv_spec(region, back):
        return pl.BlockSpec((QA_BLOCK, WIDTH),
                            lambda b, i: (b * nqb + jnp.maximum(i - back, 0), region))

    return pl.pallas_call(
        _mixa_kernel,
        grid=(batch, nqb),
        in_specs=[
            pl.BlockSpec((QA_BLOCK, WIDTH), lambda b, i: (b * nqb + i, R_QA)),
            kv_spec(R_KA, 2), kv_spec(R_KA, 1), kv_spec(R_KA, 0),
            kv_spec(R_VA, 2), kv_spec(R_VA, 1), kv_spec(R_VA, 0),
            pl.BlockSpec((N_HEADS, QA_BLOCK, QA_WIN), lambda b, i: (0, 0, 0)),
        ],
        out_specs=pl.BlockSpec((QA_BLOCK, WIDTH), lambda b, i: (b * nqb + i, 0)),
        out_shape=jax.ShapeDtypeStruct((m, WIDTH), BF16),
        compiler_params=pltpu.CompilerParams(
            dimension_semantics=("parallel", "parallel"), vmem_limit_bytes=VMEM_LIMIT),
        name="mixa",
    )(main, main, main, main, main, main, main, bias)


def _key_to_f32(t):
    return pltpu.bitcast(t ^ ((t >> 31) & jnp.int32(0x7FFFFFFF)), F32)


def _dsa_kernel(topk, qb_ref, kb_ref, vbt_ref, qi_ref, kiab_ref, wi_ref, o_ref,
                sc_ref, bias_ref, acc_ref, m_ref, l_ref):
    i = pl.program_id(1)
    tq = TQ_DSA
    nkt = i + 1
    c2 = HEAD_DIM ** -0.5 * 1.4426950408889634

    w_t = jnp.transpose(wi_ref[...]) * (IDX_HEADS ** -0.5 * IDX_DIM ** -0.5)
    qchunk = (i * tq + lax.broadcasted_iota(jnp.int32, (1, tq), 1)) // CHUNK
    row = lax.broadcasted_iota(jnp.int32, (tq, tq), 0)

    def admissible(kt):
        return (kt * tq + row) // CHUNK <= qchunk

    def idx_body(kt, carry):
        r0 = pl.multiple_of(kt * tq, tq)
        ka = kiab_ref[pl.ds(r0, tq), :LANES]
        kb = kiab_ref[pl.ds(r0, tq), LANES:]
        acc = jnp.zeros((tq, tq), F32)
        for p in range(IDX_HEADS // 2):
            qp = qi_ref[:, p * LANES:(p + 1) * LANES]
            acc = acc + w_t[2 * p:2 * p + 1, :] * jnp.maximum(_nt_dot(ka, qp), 0.0)
            acc = acc + w_t[2 * p + 1:2 * p + 2, :] * jnp.maximum(_nt_dot(kb, qp), 0.0)
        sc_ref[pl.ds(r0, tq), :] = jnp.where(admissible(kt), acc, -jnp.inf)
        return carry

    lax.fori_loop(0, nkt, idx_body, 0)

    n_adm = (qchunk + 1) * CHUNK
    need = n_adm > topk

    def count_ge(c_f):
        def cnt_body(kt, c):
            r0 = pl.multiple_of(kt * tq, tq)
            ge = jnp.where(sc_ref[pl.ds(r0, tq), :] >= c_f, 1, 0).astype(jnp.int32)
            return c + jnp.sum(ge.reshape(tq // SUBLANES, SUBLANES, tq), axis=0)

        c8 = lax.fori_loop(0, nkt, cnt_body, jnp.zeros((SUBLANES, tq), jnp.int32))
        return jnp.sum(c8, axis=0, keepdims=True)

    def bit_cond(carry):
        b, _, cnt = carry
        todo = jnp.max(jnp.where(need & (cnt != topk), 1, 0))
        return (b >= 0) & (todo > 0)

    def bit_body(carry):
        b, thr, cnt = carry
        for _ in range(BITS_PER_CHECK):
            cand = thr + lax.shift_left(jnp.int32(1), b)
            n = count_ge(_key_to_f32(cand))
            ok = n >= topk
            b, thr, cnt = b - 1, jnp.where(ok, cand, thr), jnp.where(ok, n, cnt)
        return b, thr, cnt

    init = (jnp.int32(31), jnp.full((1, tq), jnp.iinfo(jnp.int32).min, jnp.int32),
            jnp.full((1, tq), jnp.iinfo(jnp.int32).max, jnp.int32))
    _, thr, _ = lax.while_loop(bit_cond, bit_body, init)
    thr_f = _key_to_f32(thr)

    def mask_body(kt, carry):
        r0 = pl.multiple_of(kt * tq, tq)
        sel = admissible(kt) & ((sc_ref[pl.ds(r0, tq), :] >= thr_f) | jnp.logical_not(need))
        bias_ref[pl.ds(r0, tq), :] = jnp.where(sel, 0.0, NEG).astype(F32)
        return carry

    lax.fori_loop(0, nkt, mask_body, 0)

    acc_ref[...] = jnp.zeros_like(acc_ref)
    m_ref[...] = jnp.full(m_ref.shape, NEG, F32)
    l_ref[...] = jnp.zeros_like(l_ref)

    def att_body(kt, carry):
        for half in range(tq // KSUB_DSA):
            r0 = pl.multiple_of(kt * tq + half * KSUB_DSA, KSUB_DSA)
            ks = slice(half * KSUB_DSA, (half + 1) * KSUB_DSA)
            for h in range(N_HEADS):
                sl = slice(h * HEAD_DIM, (h + 1) * HEAD_DIM)
                s = _nt_dot(kb_ref[pl.ds(r0, KSUB_DSA), sl], qb_ref[:, sl]) + bias_ref[pl.ds(r0, KSUB_DSA), :]
                m_old = m_ref[h:h + 1, :]
                m_new = jnp.maximum(m_old, jnp.max(s, axis=0, keepdims=True))
                alpha = jnp.exp2((m_old - m_new) * c2)
                p = jnp.exp2((s - m_new) * c2)
                l_ref[h:h + 1, :] = alpha * l_ref[h:h + 1, :] + jnp.sum(p, axis=0, keepdims=True)
                m_ref[h:h + 1, :] = m_new
                pv = jnp.dot(vbt_ref[kt, sl, ks], p.astype(BF16), preferred_element_type=F32)
                acc_ref[h] = alpha * acc_ref[h] + pv
        return carry

    lax.fori_loop(0, nkt, att_body, 0)
    for h in range(N_HEADS):
        sl = slice(h * HEAD_DIM, (h + 1) * HEAD_DIM)
        o_ref[:, sl] = jnp.transpose(acc_ref[h] / l_ref[h:h + 1, :]).astype(BF16)


def _dsa(main, vbt, kiab, wi, batch, seq):
    m = main.shape[0]
    tq = TQ_DSA
    nq = seq // tq
    topk = min(TOPK_MAX, seq // 4)
    return pl.pallas_call(
        functools.partial(_dsa_kernel, topk),
        grid=(batch, nq),
        in_specs=[
            pl.BlockSpec((tq, WIDTH), lambda b, i: (b * nq + i, R_QB)),
            pl.BlockSpec((seq, WIDTH), lambda b, i: (b, R_KB)),
            pl.BlockSpec((nq, WIDTH, tq), lambda b, i: (b, 0, 0)),
            pl.BlockSpec((tq, WIDTH), lambda b, i: (b * nq + i, R_QI)),
            pl.BlockSpec((seq, 2 * LANES), lambda b, i: (b, 0)),
            pl.BlockSpec((tq, LANES), lambda b, i: (b * nq + i, 0)),
        ],
        out_specs=pl.BlockSpec((tq, WIDTH), lambda b, i: (b * nq + i, 0)),
        out_shape=jax.ShapeDtypeStruct((m, WIDTH), BF16),
        scratch_shapes=[pltpu.VMEM((seq, tq), F32), pltpu.VMEM((seq, tq), F32),
                        pltpu.VMEM((N_HEADS, HEAD_DIM, tq), F32),
                        pltpu.VMEM((N_HEADS, tq), F32), pltpu.VMEM((N_HEADS, tq), F32)],
        compiler_params=pltpu.CompilerParams(
            dimension_semantics=("parallel", "arbitrary"), vmem_limit_bytes=VMEM_LIMIT),
        name="dsa",
    )(main, main, vbt, main, kiab, wi)


def _outproj_kernel(oa_ref, ob_ref, x_ref, wo_ref, g_ref, h1_ref, xn_ref):
    acc = jnp.dot(oa_ref[...], wo_ref[:WIDTH, :], preferred_element_type=F32)
    acc = acc + jnp.dot(ob_ref[...], wo_ref[WIDTH:, :], preferred_element_type=F32)
    h1 = x_ref[...] + acc
    h1_ref[...] = h1
    xn_ref[...] = _rms(h1, g_ref[...]).astype(BF16)


def _outproj(out_a, out_b, x2, w_out, g_ffn):
    m = x2.shape[0]
    tm = TM_PROJ
    return pl.pallas_call(
        _outproj_kernel,
        grid=(m // tm,),
        in_specs=[
            pl.BlockSpec((tm, WIDTH), lambda i: (i, 0)),
            pl.BlockSpec((tm, WIDTH), lambda i: (i, 0)),
            pl.BlockSpec((tm, D_MODEL), lambda i: (i, 0)),
            pl.BlockSpec((2 * WIDTH, D_MODEL), lambda i: (0, 0)),
            pl.BlockSpec((1, D_MODEL), lambda i: (0, 0)),
        ],
        out_specs=[pl.BlockSpec((tm, D_MODEL), lambda i: (i, 0)), pl.BlockSpec((tm, D_MODEL), lambda i: (i, 0))],
        out_shape=[jax.ShapeDtypeStruct((m, D_MODEL), F32), jax.ShapeDtypeStruct((m, D_MODEL), BF16)],
        compiler_params=pltpu.CompilerParams(dimension_semantics=("parallel",), vmem_limit_bytes=VMEM_LIMIT),
        name="outproj",
    )(out_a, out_b, x2, w_out.astype(BF16), g_ffn)


def _merge_exchange_pairs(n):
    t = (n - 1).bit_length()
    pairs = []
    p = 1 << (t - 1)
    while p > 0:
        q, r, d = 1 << (t - 1), 0, p
        while True:
            pairs.extend((i, i + d) for i in range(n - d) if (i & p) == r)
            if q == p:
                break
            d, q, r = q - p, q >> 1, p
        p >>= 1
    return pairs


_SORT16 = _merge_exchange_pairs(PEER_TOPK)


def _tmax(a, b):
    if a is None:
        return b
    if b is None:
        return a
    return jnp.maximum(a, b)


def _cmp_exchange(x, i, j):
    a, b = x[i], x[j]
    if b is None:
        return
    if a is None:
        x[i], x[j] = b, None
        return
    x[i], x[j] = jnp.maximum(a, b), jnp.minimum(a, b)


def _sort16_desc(x):
    for i, j in _SORT16:
        _cmp_exchange(x, i, j)


def _bitonic16_desc(x):
    for d in (8, 4, 2, 1):
        for i in range(PEER_TOPK):
            if not i & d:
                _cmp_exchange(x, i, i + d)


def _merge_sublanes(x, shift, sort=True):
    y = [None if v is None else pltpu.roll(v, shift, 0) for v in x]
    out = [_tmax(x[i], y[PEER_TOPK - 1 - i]) for i in range(PEER_TOPK)]
    if sort:
        _bitonic16_desc(out)
    return out


def _top16(tiles):
    x = list(tiles)
    _sort16_desc(x)
    for shift in (4, 2, 1):
        x = _merge_sublanes(x, shift)
    return x


def _pair_threshold(v1, v2):
    sub = lax.broadcasted_iota(jnp.int32, v1[0].shape, 0)

    def diag(vals):
        d = vals[0]
        for s in range(1, SUBLANES):
            d = jnp.where(sub == s, vals[s], d)
        return d

    d1a, d1b = diag(v1[:8]), diag(v1[8:])
    d2a, d2b = diag(v2[:8]), diag(v2[8:])
    cand = [d1a + v2[0], d1b + v2[0], d1a + v2[1], d1a + v2[2], d1a + v2[3], v1[0] + d2b]
    cand += [jnp.where(sub < 4, -jnp.inf, v1[r] + d2a) for r in range(3)]
    x = cand + [None] * (PEER_TOPK - len(cand))
    _sort16_desc(x)
    x = _merge_sublanes(x, 4)
    x = _merge_sublanes(x, 2)
    x = _merge_sublanes(x, 1, sort=False)
    tau = x[0]
    for v in x[1:]:
        tau = jnp.minimum(tau, v)
    return tau


def _sublane_total(x):
    for shift in (4, 2, 1):
        x = x + pltpu.roll(x, shift, 0)
    return x


def _route_kernel(xn_ref, wq_ref, sk_ref, s2_ref, a2_ref, c_ref, a1_ref, qv_ref):
    tm = xn_ref.shape[0]
    n_t = PEER_KEYS // SUBLANES
    qv = jnp.dot(xn_ref[...], wq_ref[...], preferred_element_type=F32)
    for c in range(2 * PEER_HEADS):
        qv_ref[c] = qv[:, c * PEER_KEYS:(c + 1) * PEER_KEYS].astype(BF16)

    def chunk(h, lc):
        t0 = pl.multiple_of(lc * LANES, LANES)
        s1 = _nt_dot(sk_ref[0], qv_ref[2 * h, pl.ds(t0, LANES), :])
        s2 = _nt_dot(sk_ref[1], qv_ref[2 * h + 1, pl.ds(t0, LANES), :])
        s1 = [s1[j * SUBLANES:(j + 1) * SUBLANES, :] for j in range(n_t)]
        s2 = [s2[j * SUBLANES:(j + 1) * SUBLANES, :] for j in range(n_t)]
        v1 = _top16(s1)
        v2 = _top16(s2)
        tau = _pair_threshold(v1, v2)
        pre = []
        for q in range(PEER_TOPK):
            e = jnp.exp(v2[q] - v2[0])
            pre.append(e if q == 0 else pre[-1] + e)
        a1, cth, z = [], [], None
        for j in range(n_t):
            c_j = jnp.full(s1[j].shape, jnp.inf, F32)
            m_j = jnp.zeros(s1[j].shape, F32)
            for q in range(PEER_TOPK):
                ok = s1[j] + v2[q] >= tau
                c_j = jnp.where(ok, v2[q], c_j)
                m_j = jnp.where(ok, pre[q], m_j)
            a_j = jnp.exp(s1[j] - v1[0])
            a1.append(a_j)
            cth.append(c_j)
            z = a_j * m_j if z is None else z + a_j * m_j
        z = _sublane_total(z)
        for j in range(n_t):
            rs = slice(j * SUBLANES, (j + 1) * SUBLANES)
            s2_ref[h, lc, rs, :] = s2[j]
            a2_ref[h, lc, rs, :] = jnp.exp(s2[j] - v2[0])
            c_ref[h, lc, rs, :] = cth[j]
            a1_ref[h, lc, rs, :] = a1[j] / z

    def head_body(h, carry):
        def chunk_body(lc, c):
            chunk(h, lc)
            return c

        return lax.fori_loop(0, tm // LANES, chunk_body, carry)

    lax.fori_loop(0, PEER_HEADS, head_body, 0)


def _route(xn, w_query, sub_keys):
    m = xn.shape[0]
    tm = TM_ROUTE
    spec = pl.BlockSpec((PEER_HEADS, tm // LANES, PEER_KEYS, LANES), lambda i: (0, i, 0, 0))
    shp = jax.ShapeDtypeStruct((PEER_HEADS, m // LANES, PEER_KEYS, LANES), F32)
    return pl.pallas_call(
        _route_kernel,
        grid=(m // tm,),
        in_specs=[
            pl.BlockSpec((tm, D_MODEL), lambda i: (i, 0)),
            pl.BlockSpec((D_MODEL, 2 * PEER_HEADS * PEER_KEYS), lambda i: (0, 0)),
            pl.BlockSpec((2, PEER_KEYS, PEER_KEYS), lambda i: (0, 0, 0)),
        ],
        out_specs=[spec, spec, spec, spec],
        out_shape=[shp, shp, shp, shp],
        scratch_shapes=[pltpu.VMEM((2 * PEER_HEADS, tm, PEER_KEYS), BF16)],
        compiler_params=pltpu.CompilerParams(dimension_semantics=("parallel",), vmem_limit_bytes=VMEM_LIMIT),
        name="route",
    )(xn, w_query.astype(BF16), sub_keys.astype(BF16))


def _zero_row_from(x):
    bits = pltpu.bitcast(x, jnp.int32)
    z = lax.shift_right_logical(lax.shift_right_logical(bits, 16), 16)
    return z[0:1, :].astype(F32)


def _gate_piece(h_ref, coef_ref, p, e1_base, s2_ref, a2_ref, c_ref, a1_ref, anchors):
    n_e1 = TE_EXP // PEER_KEYS
    unit = 0
    for c in range(TP_EXP // LANES):
        lc = p * (TP_EXP // LANES) + c
        ls = slice(c * LANES, (c + 1) * LANES)
        for r0 in range(0, PEER_KEYS, GATE_ROWS):
            kr = slice(r0, r0 + GATE_ROWS)
            for e0 in range(0, n_e1, 2):
                gates = [jnp.zeros((GATE_ROWS, LANES), F32), jnp.zeros((GATE_ROWS, LANES), F32)]
                for h in range(PEER_HEADS):
                    anchor = anchors[unit * PEER_HEADS + h]
                    s2 = s2_ref[h, lc, kr, :]
                    a2 = a2_ref[h, lc, kr, :]
                    for k in range(2):
                        e1 = e1_base + e0 + k
                        cth = c_ref[h, lc, pl.ds(e1, 1), :] + anchor
                        a1 = a1_ref[h, lc, pl.ds(e1, 1), :]
                        gates[k] = gates[k] + jnp.where(s2 >= cth, a2, 0.0) * a1
                for k in range(2):
                    rs = slice((e0 + k) * PEER_KEYS + r0, (e0 + k) * PEER_KEYS + r0 + GATE_ROWS)
                    act = jax.nn.gelu(h_ref[p, rs, ls] + anchor)
                    coef_ref[p, rs, ls] = (gates[k] * act).astype(BF16)
                unit += 1


def _experts_kernel(n_steps, xn_ref, u_ref, vt_ref, s2_ref, a2_ref, c_ref, a1_ref, o_ref,
                    h0_ref, h1_ref, c0_ref, c1_ref):
    g = pl.program_id(1)
    te = TE_EXP
    n_e1 = te // PEER_KEYS
    last_e1 = (2 * n_steps - 1) * n_e1
    n_pieces = xn_ref.shape[0] // TP_EXP
    n_anchor = (TP_EXP // LANES) * (PEER_KEYS // GATE_ROWS) * (n_e1 // 2) * PEER_HEADS

    @pl.when(g == 0)
    def _():
        o_ref[...] = jnp.zeros_like(o_ref)
        h1_ref[...] = jnp.zeros_like(h1_ref)
        c0_ref[...] = jnp.zeros_like(c0_ref)

    gates = (s2_ref, a2_ref, c_ref, a1_ref)

    def half_step(u_rows, v_cols, hb_ref, hc_ref, cc_ref, ca_ref, e1_base):
        def body(p, carry):
            t0 = pl.multiple_of(p * TP_EXP, TP_EXP)
            hb_ref[p] = _nt_dot(u_ref[u_rows, :], xn_ref[pl.ds(t0, TP_EXP), :])
            res = jnp.dot(vt_ref[:, v_cols], ca_ref[p], preferred_element_type=F32)
            o_ref[p] += res
            step = (D_MODEL * ANCHOR_SPAN_PCT // 100) // n_anchor // SUBLANES * SUBLANES
            anchors = [_zero_row_from(res[k * step:k * step + SUBLANES, 0:LANES]) for k in range(n_anchor)]
            _gate_piece(hc_ref, cc_ref, p, e1_base, *gates, anchors)
            return carry

        lax.fori_loop(0, n_pieces, body, 0)

    half_step(slice(0, te), slice(0, te), h0_ref, h1_ref, c1_ref, c0_ref,
              jnp.clip((2 * g - 1) * n_e1, 0, last_e1))
    half_step(slice(te, 2 * te), slice(te, 2 * te), h1_ref, h0_ref, c0_ref, c1_ref,
              jnp.minimum(2 * g * n_e1, last_e1))


def _experts(xn, expert_u, expert_v, s2, a2, cth, a1):
    m = xn.shape[0]
    tm, te, tp = TM_EXP, TE_EXP, TP_EXP
    n_exp = expert_u.shape[0]
    n_steps = n_exp // (2 * te)
    gspec = pl.BlockSpec((PEER_HEADS, tm // LANES, PEER_KEYS, LANES), lambda i, g: (0, i, 0, 0))
    return pl.pallas_call(
        functools.partial(_experts_kernel, n_steps),
        grid=(m // tm, n_steps + 1),
        in_specs=[
            pl.BlockSpec((tm, D_MODEL), lambda i, g: (i, 0)),
            pl.BlockSpec((2 * te, D_MODEL), lambda i, g: (jnp.minimum(g, n_steps - 1), 0)),
            pl.BlockSpec((D_MODEL, 2 * te), lambda i, g: (0, jnp.maximum(g - 1, 0))),
            gspec, gspec, gspec, gspec,
        ],
        out_specs=pl.BlockSpec((tm // tp, D_MODEL, tp), lambda i, g: (i, 0, 0)),
        out_shape=jax.ShapeDtypeStruct((m // tp, D_MODEL, tp), F32),
        scratch_shapes=[pltpu.VMEM((tm // tp, te, tp), F32), pltpu.VMEM((tm // tp, te, tp), F32),
                        pltpu.VMEM((tm // tp, te, tp), BF16), pltpu.VMEM((tm // tp, te, tp), BF16)],
        compiler_params=pltpu.CompilerParams(
            dimension_semantics=("parallel", "arbitrary"), vmem_limit_bytes=VMEM_LIMIT),
        name="experts",
    )(xn, expert_u.astype(BF16), expert_v.astype(BF16).T, s2, a2, cth, a1)


def _final_kernel(h1_ref, ot_ref, p_ref, g_ref, wg_ref, wp_ref, o_ref):
    h2 = h1_ref[...] + jnp.transpose(ot_ref[0])
    hn = _rms(h2, g_ref[...]).astype(BF16)
    gate = jax.nn.sigmoid(jnp.dot(hn, wg_ref[...], preferred_element_type=F32))
    pe = jnp.dot(p_ref[...].astype(BF16), wp_ref[...], preferred_element_type=F32)
    o_ref[...] = h2 + gate * pe


def _final(h1, out_t, p2, g_ple, w_gate, w_proj):
    m = h1.shape[0]
    tm = TM_FINAL
    assert out_t.shape == (m // tm, D_MODEL, tm)
    return pl.pallas_call(
        _final_kernel,
        grid=(m // tm,),
        in_specs=[
            pl.BlockSpec((tm, D_MODEL), lambda i: (i, 0)),
            pl.BlockSpec((1, D_MODEL, tm), lambda i: (i, 0, 0)),
            pl.BlockSpec((tm, PLE_DIM), lambda i: (i, 0)),
            pl.BlockSpec((1, D_MODEL), lambda i: (0, 0)),
            pl.BlockSpec((D_MODEL, D_MODEL), lambda i: (0, 0)),
            pl.BlockSpec((PLE_DIM, D_MODEL), lambda i: (0, 0)),
        ],
        out_specs=pl.BlockSpec((tm, D_MODEL), lambda i: (i, 0)),
        out_shape=jax.ShapeDtypeStruct((m, D_MODEL), F32),
        compiler_params=pltpu.CompilerParams(dimension_semantics=("parallel",), vmem_limit_bytes=VMEM_LIMIT),
        name="final",
    )(h1, out_t, p2, g_ple, w_gate.astype(BF16), w_proj.astype(BF16))


def _layer(h2d, p2d, pos2, batch, seq, norm_mix, w_in, qk_a, qk_b, rel_bias, w_out, norm_ffn,
           peer_query, sub_keys, peer_u, peer_v, norm_ple, ple_gate, ple_proj):
    bias = _relbias_block(rel_bias)
    main, vbt, kiab, wi = _inproj(h2d, pos2, norm_mix[None, :], w_in, qk_a, qk_b)
    out_a = _mixa(main, bias, batch, seq)
    out_b = _dsa(main, vbt, kiab, wi, batch, seq)
    h1, xn = _outproj(out_a, out_b, h2d, w_out, norm_ffn[None, :])
    s2, a2, cth, a1 = _route(xn, peer_query, sub_keys)
    out_t = _experts(xn, peer_u, peer_v, s2, a2, cth, a1)
    return _final(h1, out_t, p2d, norm_ple[None, :], ple_gate, ple_proj)


def kernel(x, p, positions, norm_mix, w_in, qk_norm_a, qk_norm_b, rel_bias, w_out, norm_ffn, peer_query,
           peer_sub_keys, peer_u, peer_v, norm_ple, ple_gate, ple_proj):
    batch, seq, d = x.shape
    assert d == D_MODEL and seq % TM_PROJ == 0 and (batch * seq) % TM_PROJ == 0
    h = x.reshape(batch * seq, d)
    pos2 = positions.reshape(batch * seq, 1).astype(jnp.int32)
    for i in range(p.shape[0]):
        h = _layer(h, p[i].reshape(batch * seq, PLE_DIM), pos2, batch, seq, norm_mix[i], w_in[i],
                   qk_norm_a[i], qk_norm_b[i], rel_bias[i], w_out[i], norm_ffn[i], peer_query[i],
                   peer_sub_keys[i], peer_u[i], peer_v[i], norm_ple[i], ple_gate[i], ple_proj[i])
    return h.reshape(batch, seq, d)
```

```python
import functools

import numpy as np
import jax
import jax.numpy as jnp
from jax import lax
from jax.experimental import pallas as pl
from jax.experimental.pallas import tpu as pltpu

F32 = jnp.float32
BF16 = jnp.bfloat16

D_MODEL = 2048
HEAD_DIM = 128
N_HEADS = 8
WIDTH = N_HEADS * HEAD_DIM
CHUNK = 64
LEFT_CHUNKS = 8
MAX_REL = 128
IDX_HEADS = 16
IDX_DIM = 64
TOPK_MAX = 256
ROPE_THETA = 500000.0
ROPE_FRACTION = 4
PEER_HEADS = 8
PEER_KEYS = 128
PEER_TOPK = 16
PLE_DIM = 256
EPS = 1e-6

LANES = 128
SUBLANES = 8
VMEM_LIMIT = 56 * 1024 * 1024

NEG = -1e30

TM_PROJ = 512
QA_BLOCK = 256
QA_WIN = 768
TQ_DSA = 256
KSUB_DSA = 128
BITS_PER_CHECK = 4
TM_ROUTE = 512
TM_EXP = 512
TE_EXP = 512
TM_FINAL = 256
GATE_ROWS = 32
ANCHOR_EVERY = 2
ANCHOR_SPAN_PCT = 100
TP_EXP = 256

_NT = (((1,), (1,)), ((), ()))


def _nt_dot(a, b):
    return lax.dot_general(a, b, _NT, preferred_element_type=F32)


def _rms(x, g):
    return x * lax.rsqrt(jnp.mean(x * x, axis=-1, keepdims=True) + EPS) * g


def _rope_lane_table(group, half):
    lane = np.arange(LANES)
    m = lane % group
    tab = np.zeros((SUBLANES, LANES), np.float32)
    tab[0] = np.where(m < 2 * half, (m % half) / half, 0.0)
    tab[1] = (m < half)
    tab[2] = (m >= half) & (m < 2 * half)
    tab[3] = (m < 2 * half)
    return tab


def _rope_coeffs(pos_f, tab_ref, out_ref):
    inv_freq = tab_ref[3:4, :] / jnp.power(jnp.float32(ROPE_THETA), tab_ref[0:1, :])
    ang = pos_f * inv_freq
    s = jnp.sin(ang)
    out_ref[0] = jnp.cos(ang)
    out_ref[1] = -s * tab_ref[1:2, :]
    out_ref[2] = s * tab_ref[2:3, :]


def _apply_rope(x, coef_ref, half):
    up = pltpu.roll(x, LANES - half, 1)
    dn = pltpu.roll(x, half, 1)
    return x * coef_ref[0] + up * coef_ref[1] + dn * coef_ref[2]


def _relbias_kernel(u_ref, o_ref):
    x = jnp.broadcast_to(u_ref[0, 0:1, :], (QA_BLOCK, 1024))
    y = pltpu.roll(x, 0, 1, stride=1, stride_axis=0)[:, :QA_WIN]
    qc = lax.broadcasted_iota(jnp.int32, (QA_BLOCK, QA_WIN), 0) // CHUNK
    kc = lax.broadcasted_iota(jnp.int32, (QA_BLOCK, QA_WIN), 1) // CHUNK
    band = (kc >= qc) & (kc <= qc + LEFT_CHUNKS)
    o_ref[0] = jnp.where(band, y, NEG)


def _relbias_block(rel_bias):
    p = np.arange(1024)
    off = np.where(p < QA_WIN, p, p - 1024)
    idx = np.clip(LEFT_CHUNKS * CHUNK - off, -MAX_REL, MAX_REL) + MAX_REL
    u = jnp.take(rel_bias.astype(F32), jnp.asarray(idx, jnp.int32), axis=1)
    u = jnp.broadcast_to(u[:, None, :], (N_HEADS, SUBLANES, 1024))
    return pl.pallas_call(
        _relbias_kernel,
        grid=(N_HEADS,),
        in_specs=[pl.BlockSpec((1, SUBLANES, 1024), lambda h: (h, 0, 0))],
        out_specs=pl.BlockSpec((1, QA_BLOCK, QA_WIN), lambda h: (h, 0, 0)),
        out_shape=jax.ShapeDtypeStruct((N_HEADS, QA_BLOCK, QA_WIN), F32),
        name="relbias",
    )(u)


R_QB, R_VA, R_KB, R_KA, R_QI, R_QA = range(6)


def _inproj_kernel(x_ref, g_ref, w0_ref, w1_ref, wvt_ref, wtail_ref, qkg_ref, pos_ref, tab128_ref, tab64_ref,
                   main_ref, vbt_ref, kiab_ref, wi_ref,
                   hn_ref, acc0_ref, acc1_ref, rope128_ref, rope64_ref):
    j = pl.program_id(1)
    tm = x_ref.shape[0]

    @pl.when(j == 0)
    def _():
        hn_ref[...] = _rms(x_ref[...], g_ref[...]).astype(BF16)
        pos_f = pos_ref[...].astype(F32)
        _rope_coeffs(pos_f, tab128_ref, rope128_ref)
        _rope_coeffs(pos_f, tab64_ref, rope64_ref)
        vt = _nt_dot(wvt_ref[...], hn_ref[...]).astype(BF16)
        for t in range(tm // TQ_DSA):
            vbt_ref[t] = vt[:, t * TQ_DSA:(t + 1) * TQ_DSA]
        tail = jnp.dot(hn_ref[...], wtail_ref[...], preferred_element_type=F32)
        kk = _apply_rope(tail[:, :LANES], rope64_ref, IDX_DIM // ROPE_FRACTION // 2)
        lane = lax.broadcasted_iota(jnp.int32, (tm, LANES), 1)
        kiab_ref[:, :LANES] = jnp.where(lane < IDX_DIM, kk, 0.0).astype(BF16)
        kiab_ref[:, LANES:] = jnp.where(lane >= IDX_DIM, kk, 0.0).astype(BF16)
        wi_ref[...] = tail[:, LANES:]

    half_b = HEAD_DIM // ROPE_FRACTION // 2
    half_i = IDX_DIM // ROPE_FRACTION // 2
    epilogues = {
        R_QA: lambda a: _rms(a, qkg_ref[0:1, :]),
        R_KA: lambda a: _rms(a, qkg_ref[1:2, :]),
        R_VA: lambda a: a,
        R_QB: lambda a: _apply_rope(_rms(a, qkg_ref[2:3, :]), rope128_ref, half_b),
        R_KB: lambda a: _apply_rope(_rms(a, qkg_ref[3:4, :]), rope128_ref, half_b),
        R_QI: lambda a: _apply_rope(a, rope64_ref, half_i),
    }

    def project_pair(first):
        accs = (acc0_ref, acc1_ref)
        for r, w_ref in enumerate((w0_ref, w1_ref)):
            accs[r][...] = jnp.dot(hn_ref[...], w_ref[...], preferred_element_type=F32)
        for r in range(2):
            fn = epilogues[first + r]
            for h in range(N_HEADS):
                sl = slice(h * HEAD_DIM, (h + 1) * HEAD_DIM)
                main_ref[:, r * WIDTH + h * HEAD_DIM:r * WIDTH + (h + 1) * HEAD_DIM] = fn(accs[r][:, sl]).astype(BF16)

    for pair in range(3):
        pl.when(j == pair)(functools.partial(project_pair, 2 * pair))


def _inproj(x2, pos2, g_mix, w_in, qk_a, qk_b):
    m = x2.shape[0]
    tm = TM_PROJ
    w = w_in.astype(BF16)
    o = [0, WIDTH, 2 * WIDTH, 3 * WIDTH, 4 * WIDTH, 5 * WIDTH, 6 * WIDTH, 7 * WIDTH]
    src = {R_QA: 0, R_KA: 1, R_VA: 2, R_QB: 3, R_KB: 4, R_QI: 6}

    def w_spec(r):
        blocks = [src[2 * pair + r] for pair in range(3)]
        return pl.BlockSpec((D_MODEL, WIDTH),
                            lambda i, j: (0, jnp.where(j == 0, blocks[0], jnp.where(j == 1, blocks[1], blocks[2]))))

    w_vt = w[:, o[5]:o[6]].T
    w_ki = w[:, o[7]:o[7] + IDX_DIM]
    w_wi = w[:, o[7] + IDX_DIM:o[7] + IDX_DIM + IDX_HEADS]
    w_tail = jnp.concatenate([w_ki, w_ki, w_wi, jnp.zeros((D_MODEL, LANES - IDX_HEADS), BF16)], axis=1)
    qkg = jnp.concatenate([qk_a, qk_b, jnp.zeros((4, HEAD_DIM), F32)], axis=0).astype(F32)
    tab128 = jnp.asarray(_rope_lane_table(HEAD_DIM, HEAD_DIM // ROPE_FRACTION // 2))
    tab64 = jnp.asarray(_rope_lane_table(IDX_DIM, IDX_DIM // ROPE_FRACTION // 2))
    res = lambda shape: pl.BlockSpec(shape, lambda i, j: (0,) * len(shape))
    return pl.pallas_call(
        _inproj_kernel,
        grid=(m // tm, 3),
        in_specs=[
            pl.BlockSpec((tm, D_MODEL), lambda i, j: (i, 0)),
            res((1, D_MODEL)),
            w_spec(0),
            w_spec(1),
            res((WIDTH, D_MODEL)),
            res((D_MODEL, 2 * LANES)),
            res((SUBLANES, HEAD_DIM)),
            pl.BlockSpec((tm, 1), lambda i, j: (i, 0)),
            res((SUBLANES, LANES)),
            res((SUBLANES, LANES)),
        ],
        out_specs=[
            pl.BlockSpec((tm, 2 * WIDTH), lambda i, j: (i, j)),
            pl.BlockSpec((tm // TQ_DSA, WIDTH, TQ_DSA), lambda i, j: (i, 0, 0)),
            pl.BlockSpec((tm, 2 * LANES), lambda i, j: (i, 0)),
            pl.BlockSpec((tm, LANES), lambda i, j: (i, 0)),
        ],
        out_shape=[
            jax.ShapeDtypeStruct((m, 6 * WIDTH), BF16),
            jax.ShapeDtypeStruct((m // TQ_DSA, WIDTH, TQ_DSA), BF16),
            jax.ShapeDtypeStruct((m, 2 * LANES), BF16),
            jax.ShapeDtypeStruct((m, LANES), F32),
        ],
        scratch_shapes=[
            pltpu.VMEM((tm, D_MODEL), BF16),
            pltpu.VMEM((tm, WIDTH), F32),
            pltpu.VMEM((tm, WIDTH), F32),
            pltpu.VMEM((3, tm, LANES), F32),
            pltpu.VMEM((3, tm, LANES), F32),
        ],
        compiler_params=pltpu.CompilerParams(
            dimension_semantics=("parallel", "arbitrary"), vmem_limit_bytes=VMEM_LIMIT),
        name="inproj",
    )(x2, g_mix, w, w, w_vt, w_tail, qkg, pos2, tab128, tab64)


def _mixa_kernel(q_ref, k0_ref, k1_ref, k2_ref, v0_ref, v1_ref, v2_ref, bias_ref, o_ref):
    qb = pl.program_id(1)
    scale = HEAD_DIM ** -0.5
    k_refs = (k0_ref, k1_ref, k2_ref)
    v_refs = (v0_ref, v1_ref, v2_ref)
    pen = (jnp.where(qb >= 2, 0.0, NEG).astype(F32), jnp.where(qb >= 1, 0.0, NEG).astype(F32), None)
    for h in range(N_HEADS):
        sl = slice(h * HEAD_DIM, (h + 1) * HEAD_DIM)
        q = q_ref[:, sl]
        s = []
        for m in range(3):
            sm = _nt_dot(q, k_refs[m][:, sl]) * scale + bias_ref[h, :, m * QA_BLOCK:(m + 1) * QA_BLOCK]
            if pen[m] is not None:
                sm = sm + pen[m]
            s.append(sm)
        mx = jnp.maximum(jnp.maximum(jnp.max(s[0], axis=-1, keepdims=True), jnp.max(s[1], axis=-1, keepdims=True)),
                         jnp.max(s[2], axis=-1, keepdims=True))
        p = [jnp.exp(sm - mx) for sm in s]
        den = (jnp.sum(p[0], axis=-1, keepdims=True) + jnp.sum(p[1], axis=-1, keepdims=True)
               + jnp.sum(p[2], axis=-1, keepdims=True))
        o = jnp.dot(p[0].astype(BF16), v_refs[0][:, sl], preferred_element_type=F32)
        o = o + jnp.dot(p[1].astype(BF16), v_refs[1][:, sl], preferred_element_type=F32)
        o = o + jnp.dot(p[2].astype(BF16), v_refs[2][:, sl], preferred_element_type=F32)
        o_ref[:, sl] = (o / den).astype(BF16)


def _mixa(main, bias, batch, seq):
    m = main.shape[0]
    nqb = seq // QA_BLOCK

    def kv_spec(region, back):
        return pl.BlockSpec((QA_BLOCK, WIDTH),
                            lambda b, i: (b * nqb + jnp.maximum(i - back, 0), region))

    return pl.pallas_call(
        _mixa_kernel,
        grid=(batch, nqb),
        in_specs=[
            pl.BlockSpec((QA_BLOCK, WIDTH), lambda b, i: (b * nqb + i, R_QA)),
            kv_spec(R_KA, 2), kv_spec(R_KA, 1), kv_spec(R_KA, 0),
            kv_spec(R_VA, 2), kv_spec(R_VA, 1), kv_spec(R_VA, 0),
            pl.BlockSpec((N_HEADS, QA_BLOCK, QA_WIN), lambda b, i: (0, 0, 0)),
        ],
        out_specs=pl.BlockSpec((QA_BLOCK, WIDTH), lambda b, i: (b * nqb + i, 0)),
        out_shape=jax.ShapeDtypeStruct((m, WIDTH), BF16),
        compiler_params=pltpu.CompilerParams(
            dimension_semantics=("parallel", "parallel"), vmem_limit_bytes=VMEM_LIMIT),
        name="mixa",
    )(main, main, main, main, main, main, main, bias)


def _key_to_f32(t):
    return pltpu.bitcast(t ^ ((t >> 31) & jnp.int32(0x7FFFFFFF)), F32)


def _dsa_kernel(topk, qb_ref, kb_ref, vbt_ref, qi_ref, kiab_ref, wi_ref, o_ref,
                sc_ref, bias_ref, acc_ref, m_ref, l_ref):
    i = pl.program_id(1)
    tq = TQ_DSA
    nkt = i + 1
    c2 = HEAD_DIM ** -0.5 * 1.4426950408889634

    w_t = jnp.transpose(wi_ref[...]) * (IDX_HEADS ** -0.5 * IDX_DIM ** -0.5)
    qchunk = (i * tq + lax.broadcasted_iota(jnp.int32, (1, tq), 1)) // CHUNK
    row = lax.broadcasted_iota(jnp.int32, (tq, tq), 0)

    def admissible(kt):
        return (kt * tq + row) // CHUNK <= qchunk

    def idx_body(kt, carry):
        r0 = pl.multiple_of(kt * tq, tq)
        ka = kiab_ref[pl.ds(r0, tq), :LANES]
        kb = kiab_ref[pl.ds(r0, tq), LANES:]
        acc = jnp.zeros((tq, tq), F32)
        for p in range(IDX_HEADS // 2):
            qp = qi_ref[:, p * LANES:(p + 1) * LANES]
            acc = acc + w_t[2 * p:2 * p + 1, :] * jnp.maximum(_nt_dot(ka, qp), 0.0)
            acc = acc + w_t[2 * p + 1:2 * p + 2, :] * jnp.maximum(_nt_dot(kb, qp), 0.0)
        sc_ref[pl.ds(r0, tq), :] = jnp.where(admissible(kt), acc, -jnp.inf)
        return carry

    lax.fori_loop(0, nkt, idx_body, 0)

    n_adm = (qchunk + 1) * CHUNK
    need = n_adm > topk

    def count_ge(c_f):
        def cnt_body(kt, c):
            r0 = pl.multiple_of(kt * tq, tq)
            ge = jnp.where(sc_ref[pl.ds(r0, tq), :] >= c_f, 1, 0).astype(jnp.int32)
            return c + jnp.sum(ge.reshape(tq // SUBLANES, SUBLANES, tq), axis=0)

        c8 = lax.fori_loop(0, nkt, cnt_body, jnp.zeros((SUBLANES, tq), jnp.int32))
        return jnp.sum(c8, axis=0, keepdims=True)

    def bit_cond(carry):
        b, _, cnt = carry
        todo = jnp.max(jnp.where(need & (cnt != topk), 1, 0))
        return (b >= 0) & (todo > 0)

    def bit_body(carry):
        b, thr, cnt = carry
        for _ in range(BITS_PER_CHECK):
            cand = thr + lax.shift_left(jnp.int32(1), b)
            n = count_ge(_key_to_f32(cand))
            ok = n >= topk
            b, thr, cnt = b - 1, jnp.where(ok, cand, thr), jnp.where(ok, n, cnt)
        return b, thr, cnt

    init = (jnp.int32(31), jnp.full((1, tq), jnp.iinfo(jnp.int32).min, jnp.int32),
            jnp.full((1, tq), jnp.iinfo(jnp.int32).max, jnp.int32))
    _, thr, _ = lax.while_loop(bit_cond, bit_body, init)
    thr_f = _key_to_f32(thr)

    def mask_body(kt, carry):
        r0 = pl.multiple_of(kt * tq, tq)
        sel = admissible(kt) & ((sc_ref[pl.ds(r0, tq), :] >= thr_f) | jnp.logical_not(need))
        bias_ref[pl.ds(r0, tq), :] = jnp.where(sel, 0.0, NEG).astype(F32)
        return carry

    lax.fori_loop(0, nkt, mask_body, 0)

    acc_ref[...] = jnp.zeros_like(acc_ref)
    m_ref[...] = jnp.full(m_ref.shape, NEG, F32)
    l_ref[...] = jnp.zeros_like(l_ref)

    def att_body(kt, carry):
        for half in range(tq // KSUB_DSA):
            r0 = pl.multiple_of(kt * tq + half * KSUB_DSA, KSUB_DSA)
            ks = slice(half * KSUB_DSA, (half + 1) * KSUB_DSA)
            for h in range(N_HEADS):
                sl = slice(h * HEAD_DIM, (h + 1) * HEAD_DIM)
                s = _nt_dot(kb_ref[pl.ds(r0, KSUB_DSA), sl], qb_ref[:, sl]) + bias_ref[pl.ds(r0, KSUB_DSA), :]
                m_old = m_ref[h:h + 1, :]
                m_new = jnp.maximum(m_old, jnp.max(s, axis=0, keepdims=True))
                alpha = jnp.exp2((m_old - m_new) * c2)
                p = jnp.exp2((s - m_new) * c2)
                l_ref[h:h + 1, :] = alpha * l_ref[h:h + 1, :] + jnp.sum(p, axis=0, keepdims=True)
                m_ref[h:h + 1, :] = m_new
                pv = jnp.dot(vbt_ref[kt, sl, ks], p.astype(BF16), preferred_element_type=F32)
                acc_ref[h] = alpha * acc_ref[h] + pv
        return carry

    lax.fori_loop(0, nkt, att_body, 0)
    for h in range(N_HEADS):
        sl = slice(h * HEAD_DIM, (h + 1) * HEAD_DIM)
        o_ref[:, sl] = jnp.transpose(acc_ref[h] / l_ref[h:h + 1, :]).astype(BF16)


def _dsa(main, vbt, kiab, wi, batch, seq):
    m = main.shape[0]
    tq = TQ_DSA
    nq = seq // tq
    topk = min(TOPK_MAX, seq // 4)
    return pl.pallas_call(
        functools.partial(_dsa_kernel, topk),
        grid=(batch, nq),
        in_specs=[
            pl.BlockSpec((tq, WIDTH), lambda b, i: (b * nq + i, R_QB)),
            pl.BlockSpec((seq, WIDTH), lambda b, i: (b, R_KB)),
            pl.BlockSpec((nq, WIDTH, tq), lambda b, i: (b, 0, 0)),
            pl.BlockSpec((tq, WIDTH), lambda b, i: (b * nq + i, R_QI)),
            pl.BlockSpec((seq, 2 * LANES), lambda b, i: (b, 0)),
            pl.BlockSpec((tq, LANES), lambda b, i: (b * nq + i, 0)),
        ],
        out_specs=pl.BlockSpec((tq, WIDTH), lambda b, i: (b * nq + i, 0)),
        out_shape=jax.ShapeDtypeStruct((m, WIDTH), BF16),
        scratch_shapes=[pltpu.VMEM((seq, tq), F32), pltpu.VMEM((seq, tq), F32),
                        pltpu.VMEM((N_HEADS, HEAD_DIM, tq), F32),
                        pltpu.VMEM((N_HEADS, tq), F32), pltpu.VMEM((N_HEADS, tq), F32)],
        compiler_params=pltpu.CompilerParams(
            dimension_semantics=("parallel", "arbitrary"), vmem_limit_bytes=VMEM_LIMIT),
        name="dsa",
    )(main, main, vbt, main, kiab, wi)


def _outproj_kernel(oa_ref, ob_ref, x_ref, wo_ref, g_ref, h1_ref, xn_ref):
    acc = jnp.dot(oa_ref[...], wo_ref[:WIDTH, :], preferred_element_type=F32)
    acc = acc + jnp.dot(ob_ref[...], wo_ref[WIDTH:, :], preferred_element_type=F32)
    h1 = x_ref[...] + acc
    h1_ref[...] = h1
    xn_ref[...] = _rms(h1, g_ref[...]).astype(BF16)


def _outproj(out_a, out_b, x2, w_out, g_ffn):
    m = x2.shape[0]
    tm = TM_PROJ
    return pl.pallas_call(
        _outproj_kernel,
        grid=(m // tm,),
        in_specs=[
            pl.BlockSpec((tm, WIDTH), lambda i: (i, 0)),
            pl.BlockSpec((tm, WIDTH), lambda i: (i, 0)),
            pl.BlockSpec((tm, D_MODEL), lambda i: (i, 0)),
            pl.BlockSpec((2 * WIDTH, D_MODEL), lambda i: (0, 0)),
            pl.BlockSpec((1, D_MODEL), lambda i: (0, 0)),
        ],
        out_specs=[pl.BlockSpec((tm, D_MODEL), lambda i: (i, 0)), pl.BlockSpec((tm, D_MODEL), lambda i: (i, 0))],
        out_shape=[jax.ShapeDtypeStruct((m, D_MODEL), F32), jax.ShapeDtypeStruct((m, D_MODEL), BF16)],
        compiler_params=pltpu.CompilerParams(dimension_semantics=("parallel",), vmem_limit_bytes=VMEM_LIMIT),
        name="outproj",
    )(out_a, out_b, x2, w_out.astype(BF16), g_ffn)


def _merge_exchange_pairs(n):
    t = (n - 1).bit_length()
    pairs = []
    p = 1 << (t - 1)
    while p > 0:
        q, r, d = 1 << (t - 1), 0, p
        while True:
            pairs.extend((i, i + d) for i in range(n - d) if (i & p) == r)
            if q == p:
                break
            d, q, r = q - p, q >> 1, p
        p >>= 1
    return pairs


_SORT16 = _merge_exchange_pairs(PEER_TOPK)


def _tmax(a, b):
    if a is None:
        return b
    if b is None:
        return a
    return jnp.maximum(a, b)


def _cmp_exchange(x, i, j):
    a, b = x[i], x[j]
    if b is None:
        return
    if a is None:
        x[i], x[j] = b, None
        return
    x[i], x[j] = jnp.maximum(a, b), jnp.minimum(a, b)


def _sort16_desc(x):
    for i, j in _SORT16:
        _cmp_exchange(x, i, j)


def _bitonic16_desc(x):
    for d in (8, 4, 2, 1):
        for i in range(PEER_TOPK):
            if not i & d:
                _cmp_exchange(x, i, i + d)


def _merge_sublanes(x, shift, sort=True):
    y = [None if v is None else pltpu.roll(v, shift, 0) for v in x]
    out = [_tmax(x[i], y[PEER_TOPK - 1 - i]) for i in range(PEER_TOPK)]
    if sort:
        _bitonic16_desc(out)
    return out


def _top16(tiles):
    x = list(tiles)
    _sort16_desc(x)
    for shift in (4, 2, 1):
        x = _merge_sublanes(x, shift)
    return x


def _pair_threshold(v1, v2):
    sub = lax.broadcasted_iota(jnp.int32, v1[0].shape, 0)

    def diag(vals):
        d = vals[0]
        for s in range(1, SUBLANES):
            d = jnp.where(sub == s, vals[s], d)
        return d

    d1a, d1b = diag(v1[:8]), diag(v1[8:])
    d2a, d2b = diag(v2[:8]), diag(v2[8:])
    cand = [d1a + v2[0], d1b + v2[0], d1a + v2[1], d1a + v2[2], d1a + v2[3], v1[0] + d2b]
    cand += [jnp.where(sub < 4, -jnp.inf, v1[r] + d2a) for r in range(3)]
    x = cand + [None] * (PEER_TOPK - len(cand))
    _sort16_desc(x)
    x = _merge_sublanes(x, 4)
    x = _merge_sublanes(x, 2)
    x = _merge_sublanes(x, 1, sort=False)
    tau = x[0]
    for v in x[1:]:
        tau = jnp.minimum(tau, v)
    return tau


def _sublane_total(x):
    for shift in (4, 2, 1):
        x = x + pltpu.roll(x, shift, 0)
    return x


def _route_kernel(xn_ref, wq_ref, sk_ref, s2_ref, a2_ref, c_ref, a1_ref, qv_ref):
    tm = xn_ref.shape[0]
    n_t = PEER_KEYS // SUBLANES
    qv = jnp.dot(xn_ref[...], wq_ref[...], preferred_element_type=F32)
    for c in range(2 * PEER_HEADS):
        qv_ref[c] = qv[:, c * PEER_KEYS:(c + 1) * PEER_KEYS].astype(BF16)

    def chunk(h, lc):
        t0 = pl.multiple_of(lc * LANES, LANES)
        s1 = _nt_dot(sk_ref[0], qv_ref[2 * h, pl.ds(t0, LANES), :])
        s2 = _nt_dot(sk_ref[1], qv_ref[2 * h + 1, pl.ds(t0, LANES), :])
        s1 = [s1[j * SUBLANES:(j + 1) * SUBLANES, :] for j in range(n_t)]
        s2 = [s2[j * SUBLANES:(j + 1) * SUBLANES, :] for j in range(n_t)]
        v1 = _top16(s1)
        v2 = _top16(s2)
        tau = _pair_threshold(v1, v2)
        pre = []
        for q in range(PEER_TOPK):
            e = jnp.exp(v2[q] - v2[0])
            pre.append(e if q == 0 else pre[-1] + e)
        a1, cth, z = [], [], None
        for j in range(n_t):
            c_j = jnp.full(s1[j].shape, jnp.inf, F32)
            m_j = jnp.zeros(s1[j].shape, F32)
            for q in range(PEER_TOPK):
                ok = s1[j] + v2[q] >= tau
                c_j = jnp.where(ok, v2[q], c_j)
                m_j = jnp.where(ok, pre[q], m_j)
            a_j = jnp.exp(s1[j] - v1[0])
            a1.append(a_j)
            cth.append(c_j)
            z = a_j * m_j if z is None else z + a_j * m_j
        z = _sublane_total(z)
        for j in range(n_t):
            rs = slice(j * SUBLANES, (j + 1) * SUBLANES)
            s2_ref[h, lc, rs, :] = s2[j]
            a2_ref[h, lc, rs, :] = jnp.exp(s2[j] - v2[0])
            c_ref[h, lc, rs, :] = cth[j]
            a1_ref[h, lc, rs, :] = a1[j] / z

    def head_body(h, carry):
        def chunk_body(lc, c):
            chunk(h, lc)
            return c

        return lax.fori_loop(0, tm // LANES, chunk_body, carry)

    lax.fori_loop(0, PEER_HEADS, head_body, 0)


def _route(xn, w_query, sub_keys):
    m = xn.shape[0]
    tm = TM_ROUTE
    spec = pl.BlockSpec((PEER_HEADS, tm // LANES, PEER_KEYS, LANES), lambda i: (0, i, 0, 0))
    shp = jax.ShapeDtypeStruct((PEER_HEADS, m // LANES, PEER_KEYS, LANES), F32)
    return pl.pallas_call(
        _route_kernel,
        grid=(m // tm,),
        in_specs=[
            pl.BlockSpec((tm, D_MODEL), lambda i: (i, 0)),
            pl.BlockSpec((D_MODEL, 2 * PEER_HEADS * PEER_KEYS), lambda i: (0, 0)),
            pl.BlockSpec((2, PEER_KEYS, PEER_KEYS), lambda i: (0, 0, 0)),
        ],
        out_specs=[spec, spec, spec, spec],
        out_shape=[shp, shp, shp, shp],
        scratch_shapes=[pltpu.VMEM((2 * PEER_HEADS, tm, PEER_KEYS), BF16)],
        compiler_params=pltpu.CompilerParams(dimension_semantics=("parallel",), vmem_limit_bytes=VMEM_LIMIT),
        name="route",
    )(xn, w_query.astype(BF16), sub_keys.astype(BF16))


def _zero_row_from(x):
    bits = pltpu.bitcast(x, jnp.int32)
    z = lax.shift_right_logical(lax.shift_right_logical(bits, 16), 16)
    return z[0:1, :].astype(F32)


def _gate_piece(h_ref, coef_ref, p, e1_base, s2_ref, a2_ref, c_ref, a1_ref, anchors):
    n_e1 = TE_EXP // PEER_KEYS
    unit = 0
    for c in range(TP_EXP // LANES):
        lc = p * (TP_EXP // LANES) + c
        ls = slice(c * LANES, (c + 1) * LANES)
        for r0 in range(0, PEER_KEYS, GATE_ROWS):
            kr = slice(r0, r0 + GATE_ROWS)
            for e0 in range(0, n_e1, 2):
                gates = [jnp.zeros((GATE_ROWS, LANES), F32), jnp.zeros((GATE_ROWS, LANES), F32)]
                for h in range(PEER_HEADS):
                    anchor = anchors[(unit * PEER_HEADS + h) // ANCHOR_EVERY]
                    s2 = s2_ref[h, lc, kr, :]
                    a2 = a2_ref[h, lc, kr, :]
                    for k in range(2):
                        e1 = e1_base + e0 + k
                        cth = c_ref[h, lc, pl.ds(e1, 1), :] + anchor
                        a1 = a1_ref[h, lc, pl.ds(e1, 1), :]
                        gates[k] = gates[k] + jnp.where(s2 >= cth, a2, 0.0) * a1
                for k in range(2):
                    rs = slice((e0 + k) * PEER_KEYS + r0, (e0 + k) * PEER_KEYS + r0 + GATE_ROWS)
                    act = jax.nn.gelu(h_ref[p, rs, ls])
                    coef_ref[p, rs, ls] = (gates[k] * act).astype(BF16)
                unit += 1


def _experts_kernel(n_steps, xn_ref, u_ref, vt_ref, s2_ref, a2_ref, c_ref, a1_ref, o_ref,
                    h0_ref, h1_ref, c0_ref, c1_ref):
    g = pl.program_id(1)
    te = TE_EXP
    n_e1 = te // PEER_KEYS
    last_e1 = (2 * n_steps - 1) * n_e1
    n_pieces = xn_ref.shape[0] // TP_EXP
    n_anchor = (TP_EXP // LANES) * (PEER_KEYS // GATE_ROWS) * (n_e1 // 2) * PEER_HEADS // ANCHOR_EVERY

    @pl.when(g == 0)
    def _():
        o_ref[...] = jnp.zeros_like(o_ref)
        h1_ref[...] = jnp.zeros_like(h1_ref)
        c0_ref[...] = jnp.zeros_like(c0_ref)

    gates = (s2_ref, a2_ref, c_ref, a1_ref)

    def half_step(u_rows, v_cols, hb_ref, hc_ref, cc_ref, ca_ref, e1_base):
        def body(p, carry):
            t0 = pl.multiple_of(p * TP_EXP, TP_EXP)
            hb_ref[p] = _nt_dot(u_ref[u_rows, :], xn_ref[pl.ds(t0, TP_EXP), :])
            res = jnp.dot(vt_ref[:, v_cols], ca_ref[p], preferred_element_type=F32)
            o_ref[p] += res
            step = (D_MODEL * ANCHOR_SPAN_PCT // 100) // n_anchor // SUBLANES * SUBLANES
            anchors = [_zero_row_from(res[k * step:k * step + SUBLANES, 0:LANES]) for k in range(n_anchor)]
            _gate_piece(hc_ref, cc_ref, p, e1_base, *gates, anchors)
            return carry

        lax.fori_loop(0, n_pieces, body, 0)

    half_step(slice(0, te), slice(0, te), h0_ref, h1_ref, c1_ref, c0_ref,
              jnp.clip((2 * g - 1) * n_e1, 0, last_e1))
    half_step(slice(te, 2 * te), slice(te, 2 * te), h1_ref, h0_ref, c0_ref, c1_ref,
              jnp.minimum(2 * g * n_e1, last_e1))


def _experts(xn, expert_u, expert_v, s2, a2, cth, a1):
    m = xn.shape[0]
    tm, te, tp = TM_EXP, TE_EXP, TP_EXP
    n_exp = expert_u.shape[0]
    n_steps = n_exp // (2 * te)
    gspec = pl.BlockSpec((PEER_HEADS, tm // LANES, PEER_KEYS, LANES), lambda i, g: (0, i, 0, 0))
    return pl.pallas_call(
        functools.partial(_experts_kernel, n_steps),
        grid=(m // tm, n_steps + 1),
        in_specs=[
            pl.BlockSpec((tm, D_MODEL), lambda i, g: (i, 0)),
            pl.BlockSpec((2 * te, D_MODEL), lambda i, g: (jnp.minimum(g, n_steps - 1), 0)),
            pl.BlockSpec((D_MODEL, 2 * te), lambda i, g: (0, jnp.maximum(g - 1, 0))),
            gspec, gspec, gspec, gspec,
        ],
        out_specs=pl.BlockSpec((tm // tp, D_MODEL, tp), lambda i, g: (i, 0, 0)),
        out_shape=jax.ShapeDtypeStruct((m // tp, D_MODEL, tp), F32),
        scratch_shapes=[pltpu.VMEM((tm // tp, te, tp), F32), pltpu.VMEM((tm // tp, te, tp), F32),
                        pltpu.VMEM((tm // tp, te, tp), BF16), pltpu.VMEM((tm // tp, te, tp), BF16)],
        compiler_params=pltpu.CompilerParams(
            dimension_semantics=("parallel", "arbitrary"), vmem_limit_bytes=VMEM_LIMIT),
        name="experts",
    )(xn, expert_u.astype(BF16), expert_v.astype(BF16).T, s2, a2, cth, a1)


def _final_kernel(h1_ref, ot_ref, p_ref, g_ref, wg_ref, wp_ref, o_ref):
    h2 = h1_ref[...] + jnp.transpose(ot_ref[0])
    hn = _rms(h2, g_ref[...]).astype(BF16)
    gate = jax.nn.sigmoid(jnp.dot(hn, wg_ref[...], preferred_element_type=F32))
    pe = jnp.dot(p_ref[...].astype(BF16), wp_ref[...], preferred_element_type=F32)
    o_ref[...] = h2 + gate * pe


def _final(h1, out_t, p2, g_ple, w_gate, w_proj):
    m = h1.shape[0]
    tm = TM_FINAL
    assert out_t.shape == (m // tm, D_MODEL, tm)
    return pl.pallas_call(
        _final_kernel,
        grid=(m // tm,),
        in_specs=[
            pl.BlockSpec((tm, D_MODEL), lambda i: (i, 0)),
            pl.BlockSpec((1, D_MODEL, tm), lambda i: (i, 0, 0)),
            pl.BlockSpec((tm, PLE_DIM), lambda i: (i, 0)),
            pl.BlockSpec((1, D_MODEL), lambda i: (0, 0)),
            pl.BlockSpec((D_MODEL, D_MODEL), lambda i: (0, 0)),
            pl.BlockSpec((PLE_DIM, D_MODEL), lambda i: (0, 0)),
        ],
        out_specs=pl.BlockSpec((tm, D_MODEL), lambda i: (i, 0)),
        out_shape=jax.ShapeDtypeStruct((m, D_MODEL), F32),
        compiler_params=pltpu.CompilerParams(dimension_semantics=("parallel",), vmem_limit_bytes=VMEM_LIMIT),
        name="final",
    )(h1, out_t, p2, g_ple, w_gate.astype(BF16), w_proj.astype(BF16))


def _layer(h2d, p2d, pos2, batch, seq, norm_mix, w_in, qk_a, qk_b, rel_bias, w_out, norm_ffn,
           peer_query, sub_keys, peer_u, peer_v, norm_ple, ple_gate, ple_proj):
    bias = _relbias_block(rel_bias)
    main, vbt, kiab, wi = _inproj(h2d, pos2, norm_mix[None, :], w_in, qk_a, qk_b)
    out_a = _mixa(main, bias, batch, seq)
    out_b = _dsa(main, vbt, kiab, wi, batch, seq)
    h1, xn = _outproj(out_a, out_b, h2d, w_out, norm_ffn[None, :])
    s2, a2, cth, a1 = _route(xn, peer_query, sub_keys)
    out_t = _experts(xn, peer_u, peer_v, s2, a2, cth, a1)
    return _final(h1, out_t, p2d, norm_ple[None, :], ple_gate, ple_proj)


def kernel(x, p, positions, norm_mix, w_in, qk_norm_a, qk_norm_b, rel_bias, w_out, norm_ffn, peer_query,
           peer_sub_keys, peer_u, peer_v, norm_ple, ple_gate, ple_proj):
    batch, seq, d = x.shape
    assert d == D_MODEL and seq % TM_PROJ == 0 and (batch * seq) % TM_PROJ == 0
    h = x.reshape(batch * seq, d)
    pos2 = positions.reshape(batch * seq, 1).astype(jnp.int32)
    for i in range(p.shape[0]):
        h = _layer(h, p[i].reshape(batch * seq, PLE_DIM), pos2, batch, seq, norm_mix[i], w_in[i],
                   qk_norm_a[i], qk_norm_b[i], rel_bias[i], w_out[i], norm_ffn[i], peer_query[i],
                   peer_sub_keys[i], peer_u[i], peer_v[i], norm_ple[i], ple_gate[i], ple_proj[i])
    return h.reshape(batch, seq, d)
```

```python
import functools

import numpy as np
import jax
import jax.numpy as jnp
from jax import lax
from jax.experimental import pallas as pl
from jax.experimental.pallas import tpu as pltpu

F32 = jnp.float32
BF16 = jnp.bfloat16

D_MODEL = 2048
HEAD_DIM = 128
N_HEADS = 8
WIDTH = N_HEADS * HEAD_DIM
CHUNK = 64
LEFT_CHUNKS = 8
MAX_REL = 128
IDX_HEADS = 16
IDX_DIM = 64
TOPK_MAX = 256
ROPE_THETA = 500000.0
ROPE_FRACTION = 4
PEER_HEADS = 8
PEER_KEYS = 128
PEER_TOPK = 16
PLE_DIM = 256
EPS = 1e-6

LANES = 128
SUBLANES = 8
VMEM_LIMIT = 56 * 1024 * 1024

NEG = -1e30

TM_PROJ = 512
QA_BLOCK = 256
QA_WIN = 768
TQ_DSA = 256
KSUB_DSA = 128
BITS_PER_CHECK = 4
TM_ROUTE = 512
TM_EXP = 512
TE_EXP = 512
TM_FINAL = 256
GATE_ROWS = 32
ANCHOR_EVERY = 2
ANCHOR_SPAN_PCT = 100
TP_EXP = 256

_NT = (((1,), (1,)), ((), ()))


def _nt_dot(a, b):
    return lax.dot_general(a, b, _NT, preferred_element_type=F32)


def _rms(x, g):
    return x * lax.rsqrt(jnp.mean(x * x, axis=-1, keepdims=True) + EPS) * g


def _rope_lane_table(group, half):
    lane = np.arange(LANES)
    m = lane % group
    tab = np.zeros((SUBLANES, LANES), np.float32)
    tab[0] = np.where(m < 2 * half, (m % half) / half, 0.0)
    tab[1] = (m < half)
    tab[2] = (m >= half) & (m < 2 * half)
    tab[3] = (m < 2 * half)
    return tab


def _rope_coeffs(pos_f, tab_ref, out_ref):
    inv_freq = tab_ref[3:4, :] / jnp.power(jnp.float32(ROPE_THETA), tab_ref[0:1, :])
    ang = pos_f * inv_freq
    s = jnp.sin(ang)
    out_ref[0] = jnp.cos(ang)
    out_ref[1] = -s * tab_ref[1:2, :]
    out_ref[2] = s * tab_ref[2:3, :]


def _apply_rope(x, coef_ref, half):
    up = pltpu.roll(x, LANES - half, 1)
    dn = pltpu.roll(x, half, 1)
    return x * coef_ref[0] + up * coef_ref[1] + dn * coef_ref[2]


def _relbias_kernel(u_ref, o_ref):
    x = jnp.broadcast_to(u_ref[0, 0:1, :], (QA_BLOCK, 1024))
    y = pltpu.roll(x, 0, 1, stride=1, stride_axis=0)[:, :QA_WIN]
    qc = lax.broadcasted_iota(jnp.int32, (QA_BLOCK, QA_WIN), 0) // CHUNK
    kc = lax.broadcasted_iota(jnp.int32, (QA_BLOCK, QA_WIN), 1) // CHUNK
    band = (kc >= qc) & (kc <= qc + LEFT_CHUNKS)
    o_ref[0] = jnp.where(band, y, NEG)


def _relbias_block(rel_bias):
    p = np.arange(1024)
    off = np.where(p < QA_WIN, p, p - 1024)
    idx = np.clip(LEFT_CHUNKS * CHUNK - off, -MAX_REL, MAX_REL) + MAX_REL
    u = jnp.take(rel_bias.astype(F32), jnp.asarray(idx, jnp.int32), axis=1)
    u = jnp.broadcast_to(u[:, None, :], (N_HEADS, SUBLANES, 1024))
    return pl.pallas_call(
        _relbias_kernel,
        grid=(N_HEADS,),
        in_specs=[pl.BlockSpec((1, SUBLANES, 1024), lambda h: (h, 0, 0))],
        out_specs=pl.BlockSpec((1, QA_BLOCK, QA_WIN), lambda h: (h, 0, 0)),
        out_shape=jax.ShapeDtypeStruct((N_HEADS, QA_BLOCK, QA_WIN), F32),
        name="relbias",
    )(u)


R_QB, R_VA, R_KB, R_KA, R_QI, R_QA = range(6)


def _inproj_kernel(x_ref, g_ref, w0_ref, w1_ref, wvt_ref, wtail_ref, qkg_ref, pos_ref, tab128_ref, tab64_ref,
                   main_ref, vbt_ref, kiab_ref, wi_ref,
                   hn_ref, acc0_ref, acc1_ref, rope128_ref, rope64_ref):
    j = pl.program_id(1)
    tm = x_ref.shape[0]

    @pl.when(j == 0)
    def _():
        hn_ref[...] = _rms(x_ref[...], g_ref[...]).astype(BF16)
        pos_f = pos_ref[...].astype(F32)
        _rope_coeffs(pos_f, tab128_ref, rope128_ref)
        _rope_coeffs(pos_f, tab64_ref, rope64_ref)
        vt = _nt_dot(wvt_ref[...], hn_ref[...]).astype(BF16)
        for t in range(tm // TQ_DSA):
            vbt_ref[t] = vt[:, t * TQ_DSA:(t + 1) * TQ_DSA]
        tail = jnp.dot(hn_ref[...], wtail_ref[...], preferred_element_type=F32)
        kk = _apply_rope(tail[:, :LANES], rope64_ref, IDX_DIM // ROPE_FRACTION // 2)
        lane = lax.broadcasted_iota(jnp.int32, (tm, LANES), 1)
        kiab_ref[:, :LANES] = jnp.where(lane < IDX_DIM, kk, 0.0).astype(BF16)
        kiab_ref[:, LANES:] = jnp.where(lane >= IDX_DIM, kk, 0.0).astype(BF16)
        wi_ref[...] = tail[:, LANES:]

    half_b = HEAD_DIM // ROPE_FRACTION // 2
    half_i = IDX_DIM // ROPE_FRACTION // 2
    epilogues = {
        R_QA: lambda a: _rms(a, qkg_ref[0:1, :]),
        R_KA: lambda a: _rms(a, qkg_ref[1:2, :]),
        R_VA: lambda a: a,
        R_QB: lambda a: _apply_rope(_rms(a, qkg_ref[2:3, :]), rope128_ref, half_b),
        R_KB: lambda a: _apply_rope(_rms(a, qkg_ref[3:4, :]), rope128_ref, half_b),
        R_QI: lambda a: _apply_rope(a, rope64_ref, half_i),
    }

    def project_pair(first):
        accs = (acc0_ref, acc1_ref)
        for r, w_ref in enumerate((w0_ref, w1_ref)):
            accs[r][...] = jnp.dot(hn_ref[...], w_ref[...], preferred_element_type=F32)
        for r in range(2):
            fn = epilogues[first + r]
            for h in range(N_HEADS):
                sl = slice(h * HEAD_DIM, (h + 1) * HEAD_DIM)
                main_ref[:, r * WIDTH + h * HEAD_DIM:r * WIDTH + (h + 1) * HEAD_DIM] = fn(accs[r][:, sl]).astype(BF16)

    for pair in range(3):
        pl.when(j == pair)(functools.partial(project_pair, 2 * pair))


def _inproj(x2, pos2, g_mix, w_in, qk_a, qk_b):
    m = x2.shape[0]
    tm = TM_PROJ
    w = w_in.astype(BF16)
    o = [0, WIDTH, 2 * WIDTH, 3 * WIDTH, 4 * WIDTH, 5 * WIDTH, 6 * WIDTH, 7 * WIDTH]
    src = {R_QA: 0, R_KA: 1, R_VA: 2, R_QB: 3, R_KB: 4, R_QI: 6}

    def w_spec(r):
        blocks = [src[2 * pair + r] for pair in range(3)]
        return pl.BlockSpec((D_MODEL, WIDTH),
                            lambda i, j: (0, jnp.where(j == 0, blocks[0], jnp.where(j == 1, blocks[1], blocks[2]))))

    w_vt = w[:, o[5]:o[6]].T
    w_ki = w[:, o[7]:o[7] + IDX_DIM]
    w_wi = w[:, o[7] + IDX_DIM:o[7] + IDX_DIM + IDX_HEADS]
    w_tail = jnp.concatenate([w_ki, w_ki, w_wi, jnp.zeros((D_MODEL, LANES - IDX_HEADS), BF16)], axis=1)
    qkg = jnp.concatenate([qk_a, qk_b, jnp.zeros((4, HEAD_DIM), F32)], axis=0).astype(F32)
    tab128 = jnp.asarray(_rope_lane_table(HEAD_DIM, HEAD_DIM // ROPE_FRACTION // 2))
    tab64 = jnp.asarray(_rope_lane_table(IDX_DIM, IDX_DIM // ROPE_FRACTION // 2))
    res = lambda shape: pl.BlockSpec(shape, lambda i, j: (0,) * len(shape))
    return pl.pallas_call(
        _inproj_kernel,
        grid=(m // tm, 3),
        in_specs=[
            pl.BlockSpec((tm, D_MODEL), lambda i, j: (i, 0)),
            res((1, D_MODEL)),
            w_spec(0),
            w_spec(1),
            res((WIDTH, D_MODEL)),
            res((D_MODEL, 2 * LANES)),
            res((SUBLANES, HEAD_DIM)),
            pl.BlockSpec((tm, 1), lambda i, j: (i, 0)),
            res((SUBLANES, LANES)),
            res((SUBLANES, LANES)),
        ],
        out_specs=[
            pl.BlockSpec((tm, 2 * WIDTH), lambda i, j: (i, j)),
            pl.BlockSpec((tm // TQ_DSA, WIDTH, TQ_DSA), lambda i, j: (i, 0, 0)),
            pl.BlockSpec((tm, 2 * LANES), lambda i, j: (i, 0)),
            pl.BlockSpec((tm, LANES), lambda i, j: (i, 0)),
        ],
        out_shape=[
            jax.ShapeDtypeStruct((m, 6 * WIDTH), BF16),
            jax.ShapeDtypeStruct((m // TQ_DSA, WIDTH, TQ_DSA), BF16),
            jax.ShapeDtypeStruct((m, 2 * LANES), BF16),
            jax.ShapeDtypeStruct((m, LANES), F32),
        ],
        scratch_shapes=[
            pltpu.VMEM((tm, D_MODEL), BF16),
            pltpu.VMEM((tm, WIDTH), F32),
            pltpu.VMEM((tm, WIDTH), F32),
            pltpu.VMEM((3, tm, LANES), F32),
            pltpu.VMEM((3, tm, LANES), F32),
        ],
        compiler_params=pltpu.CompilerParams(
            dimension_semantics=("parallel", "arbitrary"), vmem_limit_bytes=VMEM_LIMIT),
        name="inproj",
    )(x2, g_mix, w, w, w_vt, w_tail, qkg, pos2, tab128, tab64)


def _mixa_kernel(q_ref, k0_ref, k1_ref, k2_ref, v0_ref, v1_ref, v2_ref, bias_ref, o_ref):
    qb = pl.program_id(1)
    scale = HEAD_DIM ** -0.5
    k_refs = (k0_ref, k1_ref, k2_ref)
    v_refs = (v0_ref, v1_ref, v2_ref)
    pen = (jnp.where(qb >= 2, 0.0, NEG).astype(F32), jnp.where(qb >= 1, 0.0, NEG).astype(F32), None)
    for h in range(N_HEADS):
        sl = slice(h * HEAD_DIM, (h + 1) * HEAD_DIM)
        q = q_ref[:, sl]
        s = []
        for m in range(3):
            sm = _nt_dot(q, k_refs[m][:, sl]) * scale + bias_ref[h, :, m * QA_BLOCK:(m + 1) * QA_BLOCK]
            if pen[m] is not None:
                sm = sm + pen[m]
            s.append(sm)
        mx = jnp.maximum(jnp.maximum(jnp.max(s[0], axis=-1, keepdims=True), jnp.max(s[1], axis=-1, keepdims=True)),
                         jnp.max(s[2], axis=-1, keepdims=True))
        p = [jnp.exp(sm - mx) for sm in s]
        den = (jnp.sum(p[0], axis=-1, keepdims=True) + jnp.sum(p[1], axis=-1, keepdims=True)
               + jnp.sum(p[2], axis=-1, keepdims=True))
        o = jnp.dot(p[0].astype(BF16), v_refs[0][:, sl], preferred_element_type=F32)
        o = o + jnp.dot(p[1].astype(BF16), v_refs[1][:, sl], preferred_element_type=F32)
        o = o + jnp.dot(p[2].astype(BF16), v_refs[2][:, sl], preferred_element_type=F32)
        o_ref[:, sl] = (o / den).astype(BF16)


def _mixa(main, bias, batch, seq):
    m = main.shape[0]
    nqb = seq // QA_BLOCK

    def kv_spec(region, back):
        return pl.BlockSpec((QA_BLOCK, WIDTH),
                            lambda b, i: (b * nqb + jnp.maximum(i - back, 0), region))

    return pl.pallas_call(
        _mixa_kernel,
        grid=(batch, nqb),
        in_specs=[
            pl.BlockSpec((QA_BLOCK, WIDTH), lambda b, i: (b * nqb + i, R_QA)),
            kv_spec(R_KA, 2), kv_spec(R_KA, 1), kv_spec(R_KA, 0),
            kv_spec(R_VA, 2), kv_spec(R_VA, 1), kv_spec(R_VA, 0),
            pl.BlockSpec((N_HEADS, QA_BLOCK, QA_WIN), lambda b, i: (0, 0, 0)),
        ],
        out_specs=pl.BlockSpec((QA_BLOCK, WIDTH), lambda b, i: (b * nqb + i, 0)),
        out_shape=jax.ShapeDtypeStruct((m, WIDTH), BF16),
        compiler_params=pltpu.CompilerParams(
            dimension_semantics=("parallel", "parallel"), vmem_limit_bytes=VMEM_LIMIT),
        name="mixa",
    )(main, main, main, main, main, main, main, bias)


def _key_to_f32(t):
    return pltpu.bitcast(t ^ ((t >> 31) & jnp.int32(0x7FFFFFFF)), F32)


def _dsa_kernel(topk, qb_ref, kb_ref, vbt_ref, qi_ref, kiab_ref, wi_ref, o_ref,
                sc_ref, bias_ref, acc_ref, m_ref, l_ref):
    i = pl.program_id(1)
    tq = TQ_DSA
    nkt = i + 1
    c2 = HEAD_DIM ** -0.5 * 1.4426950408889634

    w_t = jnp.transpose(wi_ref[...]) * (IDX_HEADS ** -0.5 * IDX_DIM ** -0.5)
    qchunk = (i * tq + lax.broadcasted_iota(jnp.int32, (1, tq), 1)) // CHUNK
    row = lax.broadcasted_iota(jnp.int32, (tq, tq), 0)

    def admissible(kt):
        return (kt * tq + row) // CHUNK <= qchunk

    def idx_body(kt, carry):
        r0 = pl.multiple_of(kt * tq, tq)
        ka = kiab_ref[pl.ds(r0, tq), :LANES]
        kb = kiab_ref[pl.ds(r0, tq), LANES:]
        acc = jnp.zeros((tq, tq), F32)
        for p in range(IDX_HEADS // 2):
            qp = qi_ref[:, p * LANES:(p + 1) * LANES]
            acc = acc + w_t[2 * p:2 * p + 1, :] * jnp.maximum(_nt_dot(ka, qp), 0.0)
            acc = acc + w_t[2 * p + 1:2 * p + 2, :] * jnp.maximum(_nt_dot(kb, qp), 0.0)
        sc_ref[pl.ds(r0, tq), :] = jnp.where(admissible(kt), acc, -jnp.inf)
        return carry

    lax.fori_loop(0, nkt, idx_body, 0)

    n_adm = (qchunk + 1) * CHUNK
    need = n_adm > topk

    def count_ge(c_f):
        def cnt_body(kt, c):
            r0 = pl.multiple_of(kt * tq, tq)
            ge = jnp.where(sc_ref[pl.ds(r0, tq), :] >= c_f, 1, 0).astype(jnp.int32)
            return c + jnp.sum(ge.reshape(tq // SUBLANES, SUBLANES, tq), axis=0)

        c8 = lax.fori_loop(0, nkt, cnt_body, jnp.zeros((SUBLANES, tq), jnp.int32))
        return jnp.sum(c8, axis=0, keepdims=True)

    def bit_cond(carry):
        b, _, cnt = carry
        todo = jnp.max(jnp.where(need & (cnt != topk), 1, 0))
        return (b >= 0) & (todo > 0)

    def bit_body(carry):
        b, thr, cnt = carry
        for _ in range(BITS_PER_CHECK):
            cand = thr + lax.shift_left(jnp.int32(1), b)
            n = count_ge(_key_to_f32(cand))
            ok = n >= topk
            b, thr, cnt = b - 1, jnp.where(ok, cand, thr), jnp.where(ok, n, cnt)
        return b, thr, cnt

    init = (jnp.int32(31), jnp.full((1, tq), jnp.iinfo(jnp.int32).min, jnp.int32),
            jnp.full((1, tq), jnp.iinfo(jnp.int32).max, jnp.int32))
    _, thr, _ = lax.while_loop(bit_cond, bit_body, init)
    thr_f = _key_to_f32(thr)

    def mask_body(kt, carry):
        r0 = pl.multiple_of(kt * tq, tq)
        sel = admissible(kt) & ((sc_ref[pl.ds(r0, tq), :] >= thr_f) | jnp.logical_not(need))
        bias_ref[pl.ds(r0, tq), :] = jnp.where(sel, 0.0, NEG).astype(F32)
        return carry

    lax.fori_loop(0, nkt, mask_body, 0)

    acc_ref[...] = jnp.zeros_like(acc_ref)
    m_ref[...] = jnp.full(m_ref.shape, NEG, F32)
    l_ref[...] = jnp.zeros_like(l_ref)

    def att_body(kt, carry):
        for half in range(tq // KSUB_DSA):
            r0 = pl.multiple_of(kt * tq + half * KSUB_DSA, KSUB_DSA)
            ks = slice(half * KSUB_DSA, (half + 1) * KSUB_DSA)
            for h in range(N_HEADS):
                sl = slice(h * HEAD_DIM, (h + 1) * HEAD_DIM)
                s = _nt_dot(kb_ref[pl.ds(r0, KSUB_DSA), sl], qb_ref[:, sl]) + bias_ref[pl.ds(r0, KSUB_DSA), :]
                m_old = m_ref[h:h + 1, :]
                m_new = jnp.maximum(m_old, jnp.max(s, axis=0, keepdims=True))
                alpha = jnp.exp2((m_old - m_new) * c2)
                p = jnp.exp2((s - m_new) * c2)
                l_ref[h:h + 1, :] = alpha * l_ref[h:h + 1, :] + jnp.sum(p, axis=0, keepdims=True)
                m_ref[h:h + 1, :] = m_new
                pv = jnp.dot(vbt_ref[kt, sl, ks], p.astype(BF16), preferred_element_type=F32)
                acc_ref[h] = alpha * acc_ref[h] + pv
        return carry

    lax.fori_loop(0, nkt, att_body, 0)
    for h in range(N_HEADS):
        sl = slice(h * HEAD_DIM, (h + 1) * HEAD_DIM)
        o_ref[:, sl] = jnp.transpose(acc_ref[h] / l_ref[h:h + 1, :]).astype(BF16)


def _dsa(main, vbt, kiab, wi, batch, seq):
    m = main.shape[0]
    tq = TQ_DSA
    nq = seq // tq
    topk = min(TOPK_MAX, seq // 4)
    return pl.pallas_call(
        functools.partial(_dsa_kernel, topk),
        grid=(batch, nq),
        in_specs=[
            pl.BlockSpec((tq, WIDTH), lambda b, i: (b * nq + i, R_QB)),
            pl.BlockSpec((seq, WIDTH), lambda b, i: (b, R_KB)),
            pl.BlockSpec((nq, WIDTH, tq), lambda b, i: (b, 0, 0)),
            pl.BlockSpec((tq, WIDTH), lambda b, i: (b * nq + i, R_QI)),
            pl.BlockSpec((seq, 2 * LANES), lambda b, i: (b, 0)),
            pl.BlockSpec((tq, LANES), lambda b, i: (b * nq + i, 0)),
        ],
        out_specs=pl.BlockSpec((tq, WIDTH), lambda b, i: (b * nq + i, 0)),
        out_shape=jax.ShapeDtypeStruct((m, WIDTH), BF16),
        scratch_shapes=[pltpu.VMEM((seq, tq), F32), pltpu.VMEM((seq, tq), F32),
                        pltpu.VMEM((N_HEADS, HEAD_DIM, tq), F32),
                        pltpu.VMEM((N_HEADS, tq), F32), pltpu.VMEM((N_HEADS, tq), F32)],
        compiler_params=pltpu.CompilerParams(
            dimension_semantics=("parallel", "arbitrary"), vmem_limit_bytes=VMEM_LIMIT),
        name="dsa",
    )(main, main, vbt, main, kiab, wi)


def _outproj_kernel(oa_ref, ob_ref, x_ref, wo_ref, g_ref, h1_ref, xn_ref):
    acc = jnp.dot(oa_ref[...], wo_ref[:WIDTH, :], preferred_element_type=F32)
    acc = acc + jnp.dot(ob_ref[...], wo_ref[WIDTH:, :], preferred_element_type=F32)
    h1 = x_ref[...] + acc
    h1_ref[...] = h1
    xn_ref[...] = _rms(h1, g_ref[...]).astype(BF16)


def _outproj(out_a, out_b, x2, w_out, g_ffn):
    m = x2.shape[0]
    tm = TM_PROJ
    return pl.pallas_call(
        _outproj_kernel,
        grid=(m // tm,),
        in_specs=[
            pl.BlockSpec((tm, WIDTH), lambda i: (i, 0)),
            pl.BlockSpec((tm, WIDTH), lambda i: (i, 0)),
            pl.BlockSpec((tm, D_MODEL), lambda i: (i, 0)),
            pl.BlockSpec((2 * WIDTH, D_MODEL), lambda i: (0, 0)),
            pl.BlockSpec((1, D_MODEL), lambda i: (0, 0)),
        ],
        out_specs=[pl.BlockSpec((tm, D_MODEL), lambda i: (i, 0)), pl.BlockSpec((tm, D_MODEL), lambda i: (i, 0))],
        out_shape=[jax.ShapeDtypeStruct((m, D_MODEL), F32), jax.ShapeDtypeStruct((m, D_MODEL), BF16)],
        compiler_params=pltpu.CompilerParams(dimension_semantics=("parallel",), vmem_limit_bytes=VMEM_LIMIT),
        name="outproj",
    )(out_a, out_b, x2, w_out.astype(BF16), g_ffn)


def _merge_exchange_pairs(n):
    t = (n - 1).bit_length()
    pairs = []
    p = 1 << (t - 1)
    while p > 0:
        q, r, d = 1 << (t - 1), 0, p
        while True:
            pairs.extend((i, i + d) for i in range(n - d) if (i & p) == r)
            if q == p:
                break
            d, q, r = q - p, q >> 1, p
        p >>= 1
    return pairs


_SORT16 = _merge_exchange_pairs(PEER_TOPK)


def _tmax(a, b):
    if a is None:
        return b
    if b is None:
        return a
    return jnp.maximum(a, b)


def _cmp_exchange(x, i, j):
    a, b = x[i], x[j]
    if b is None:
        return
    if a is None:
        x[i], x[j] = b, None
        return
    x[i], x[j] = jnp.maximum(a, b), jnp.minimum(a, b)


def _sort16_desc(x):
    for i, j in _SORT16:
        _cmp_exchange(x, i, j)


def _bitonic16_desc(x):
    for d in (8, 4, 2, 1):
        for i in range(PEER_TOPK):
            if not i & d:
                _cmp_exchange(x, i, i + d)


def _merge_sublanes(x, shift, sort=True):
    y = [None if v is None else pltpu.roll(v, shift, 0) for v in x]
    out = [_tmax(x[i], y[PEER_TOPK - 1 - i]) for i in range(PEER_TOPK)]
    if sort:
        _bitonic16_desc(out)
    return out


def _top16(tiles):
    x = list(tiles)
    _sort16_desc(x)
    for shift in (4, 2, 1):
        x = _merge_sublanes(x, shift)
    return x


def _pair_threshold(v1, v2):
    sub = lax.broadcasted_iota(jnp.int32, v1[0].shape, 0)

    def diag(vals):
        d = vals[0]
        for s in range(1, SUBLANES):
            d = jnp.where(sub == s, vals[s], d)
        return d

    d1a, d1b = diag(v1[:8]), diag(v1[8:])
    d2a, d2b = diag(v2[:8]), diag(v2[8:])
    cand = [d1a + v2[0], d1b + v2[0], d1a + v2[1], d1a + v2[2], d1a + v2[3], v1[0] + d2b]
    cand += [jnp.where(sub < 4, -jnp.inf, v1[r] + d2a) for r in range(3)]
    x = cand + [None] * (PEER_TOPK - len(cand))
    _sort16_desc(x)
    x = _merge_sublanes(x, 4)
    x = _merge_sublanes(x, 2)
    x = _merge_sublanes(x, 1, sort=False)
    tau = x[0]
    for v in x[1:]:
        tau = jnp.minimum(tau, v)
    return tau


def _sublane_total(x):
    for shift in (4, 2, 1):
        x = x + pltpu.roll(x, shift, 0)
    return x


def _route_kernel(xn_ref, wq_ref, sk_ref, s2_ref, a2_ref, c_ref, a1_ref, qv_ref):
    tm = xn_ref.shape[0]
    n_t = PEER_KEYS // SUBLANES
    qv = jnp.dot(xn_ref[...], wq_ref[...], preferred_element_type=F32)
    for c in range(2 * PEER_HEADS):
        qv_ref[c] = qv[:, c * PEER_KEYS:(c + 1) * PEER_KEYS].astype(BF16)

    def chunk(h, lc):
        t0 = pl.multiple_of(lc * LANES, LANES)
        s1 = _nt_dot(sk_ref[0], qv_ref[2 * h, pl.ds(t0, LANES), :])
        s2 = _nt_dot(sk_ref[1], qv_ref[2 * h + 1, pl.ds(t0, LANES), :])
        s1 = [s1[j * SUBLANES:(j + 1) * SUBLANES, :] for j in range(n_t)]
        s2 = [s2[j * SUBLANES:(j + 1) * SUBLANES, :] for j in range(n_t)]
        v1 = _top16(s1)
        v2 = _top16(s2)
        tau = _pair_threshold(v1, v2)
        pre = []
        for q in range(PEER_TOPK):
            e = jnp.exp(v2[q] - v2[0])
            pre.append(e if q == 0 else pre[-1] + e)
        a1, cth, z = [], [], None
        for j in range(n_t):
            c_j = jnp.full(s1[j].shape, jnp.inf, F32)
            m_j = jnp.zeros(s1[j].shape, F32)
            for q in range(PEER_TOPK):
                ok = s1[j] + v2[q] >= tau
                c_j = jnp.where(ok, v2[q], c_j)
                m_j = jnp.where(ok, pre[q], m_j)
            a_j = jnp.exp(s1[j] - v1[0])
            a1.append(a_j)
            cth.append(c_j)
            z = a_j * m_j if z is None else z + a_j * m_j
        z = _sublane_total(z)
        for j in range(n_t):
            rs = slice(j * SUBLANES, (j + 1) * SUBLANES)
            s2_ref[h, lc, rs, :] = s2[j]
            a2_ref[h, lc, rs, :] = jnp.exp(s2[j] - v2[0])
            c_ref[h, lc, rs, :] = cth[j]
            a1_ref[h, lc, rs, :] = a1[j] / z

    def head_body(h, carry):
        def chunk_body(lc, c):
            chunk(h, lc)
            return c

        return lax.fori_loop(0, tm // LANES, chunk_body, carry)

    lax.fori_loop(0, PEER_HEADS, head_body, 0)


def _route(xn, w_query, sub_keys):
    m = xn.shape[0]
    tm = TM_ROUTE
    spec = pl.BlockSpec((PEER_HEADS, tm // LANES, PEER_KEYS, LANES), lambda i: (0, i, 0, 0))
    shp = jax.ShapeDtypeStruct((PEER_HEADS, m // LANES, PEER_KEYS, LANES), F32)
    return pl.pallas_call(
        _route_kernel,
        grid=(m // tm,),
        in_specs=[
            pl.BlockSpec((tm, D_MODEL), lambda i: (i, 0)),
            pl.BlockSpec((D_MODEL, 2 * PEER_HEADS * PEER_KEYS), lambda i: (0, 0)),
            pl.BlockSpec((2, PEER_KEYS, PEER_KEYS), lambda i: (0, 0, 0)),
        ],
        out_specs=[spec, spec, spec, spec],
        out_shape=[shp, shp, shp, shp],
        scratch_shapes=[pltpu.VMEM((2 * PEER_HEADS, tm, PEER_KEYS), BF16)],
        compiler_params=pltpu.CompilerParams(dimension_semantics=("parallel",), vmem_limit_bytes=VMEM_LIMIT),
        name="route",
    )(xn, w_query.astype(BF16), sub_keys.astype(BF16))


def _zero_row_from(x):
    bits = pltpu.bitcast(x, jnp.int32)
    z = lax.shift_right_logical(lax.shift_right_logical(bits, 16), 16)
    return z[0:1, :].astype(F32)


def _gate_piece(h_ref, coef_ref, p, e1_base, s2_ref, a2_ref, c_ref, a1_ref, anchors):
    n_e1 = TE_EXP // PEER_KEYS
    unit = 0
    for c in range(TP_EXP // LANES):
        lc = p * (TP_EXP // LANES) + c
        ls = slice(c * LANES, (c + 1) * LANES)
        for r0 in range(0, PEER_KEYS, GATE_ROWS):
            kr = slice(r0, r0 + GATE_ROWS)
            for e0 in range(0, n_e1, 2):
                gates = [jnp.zeros((GATE_ROWS, LANES), F32), jnp.zeros((GATE_ROWS, LANES), F32)]
                for h in range(PEER_HEADS):
                    anchor = anchors[(unit * PEER_HEADS + h) // ANCHOR_EVERY]
                    s2 = s2_ref[h, lc, kr, :]
                    a2 = a2_ref[h, lc, kr, :]
                    for k in range(2):
                        e1 = e1_base + e0 + k
                        cth = c_ref[h, lc, pl.ds(e1, 1), :] + anchor
                        a1 = a1_ref[h, lc, pl.ds(e1, 1), :]
                        gates[k] = gates[k] + jnp.where(s2 >= cth, a2, 0.0) * a1
                for k in range(2):
                    rs = slice((e0 + k) * PEER_KEYS + r0, (e0 + k) * PEER_KEYS + r0 + GATE_ROWS)
                    act = jax.nn.gelu(h_ref[p, rs, ls])
                    coef_ref[p, rs, ls] = (gates[k] * act).astype(BF16)
                unit += 1


def _experts_kernel(n_steps, xn_ref, u_ref, vt_ref, s2_ref, a2_ref, c_ref, a1_ref, o_ref,
                    h0_ref, h1_ref, c0_ref, c1_ref):
    g = pl.program_id(1)
    te = TE_EXP
    n_e1 = te // PEER_KEYS
    last_e1 = (2 * n_steps - 1) * n_e1
    n_pieces = xn_ref.shape[0] // TP_EXP
    n_anchor = (TP_EXP // LANES) * (PEER_KEYS // GATE_ROWS) * (n_e1 // 2) * PEER_HEADS // ANCHOR_EVERY

    @pl.when(g == 0)
    def _():
        o_ref[...] = jnp.zeros_like(o_ref)

    gates = (s2_ref, a2_ref, c_ref, a1_ref)

    def half_step(u_rows, v_cols, hb_ref, hc_ref, cc_ref, ca_ref, e1_base, do_b, do_c, do_a):
        def body(p, carry):
            t0 = pl.multiple_of(p * TP_EXP, TP_EXP)
            if do_b:
                hb_ref[p] = _nt_dot(u_ref[u_rows, :], xn_ref[pl.ds(t0, TP_EXP), :])
            if do_a:
                res = jnp.dot(vt_ref[:, v_cols], ca_ref[p], preferred_element_type=F32)
                o_ref[p] += res
            if do_c:
                if do_a:
                    step = (D_MODEL * ANCHOR_SPAN_PCT // 100) // n_anchor // SUBLANES * SUBLANES
                    anchors = [_zero_row_from(res[k * step:k * step + SUBLANES, 0:LANES]) for k in range(n_anchor)]
                else:
                    anchors = [jnp.zeros((1, LANES), F32)] * n_anchor
                _gate_piece(hc_ref, cc_ref, p, e1_base, *gates, anchors)
            return carry

        lax.fori_loop(0, n_pieces, body, 0)

    def step(first, second):
        half_step(slice(0, te), slice(0, te), h0_ref, h1_ref, c1_ref, c0_ref,
                  jnp.clip((2 * g - 1) * n_e1, 0, last_e1), *first)
        half_step(slice(te, 2 * te), slice(te, 2 * te), h1_ref, h0_ref, c0_ref, c1_ref,
                  jnp.minimum(2 * g * n_e1, last_e1), *second)

    pl.when(g == 0)(functools.partial(step, (True, False, False), (True, True, False)))
    pl.when((g > 0) & (g < n_steps))(functools.partial(step, (True, True, True), (True, True, True)))
    pl.when(g == n_steps)(functools.partial(step, (False, True, True), (False, False, True)))


def _experts(xn, expert_u, expert_v, s2, a2, cth, a1):
    m = xn.shape[0]
    tm, te, tp = TM_EXP, TE_EXP, TP_EXP
    n_exp = expert_u.shape[0]
    n_steps = n_exp // (2 * te)
    gspec = pl.BlockSpec((PEER_HEADS, tm // LANES, PEER_KEYS, LANES), lambda i, g: (0, i, 0, 0))
    return pl.pallas_call(
        functools.partial(_experts_kernel, n_steps),
        grid=(m // tm, n_steps + 1),
        in_specs=[
            pl.BlockSpec((tm, D_MODEL), lambda i, g: (i, 0)),
            pl.BlockSpec((2 * te, D_MODEL), lambda i, g: (jnp.minimum(g, n_steps - 1), 0)),
            pl.BlockSpec((D_MODEL, 2 * te), lambda i, g: (0, jnp.maximum(g - 1, 0))),
            gspec, gspec, gspec, gspec,
        ],
        out_specs=pl.BlockSpec((tm // tp, D_MODEL, tp), lambda i, g: (i, 0, 0)),
        out_shape=jax.ShapeDtypeStruct((m // tp, D_MODEL, tp), F32),
        scratch_shapes=[pltpu.VMEM((tm // tp, te, tp), F32), pltpu.VMEM((tm // tp, te, tp), F32),
                        pltpu.VMEM((tm // tp, te, tp), BF16), pltpu.VMEM((tm // tp, te, tp), BF16)],
        compiler_params=pltpu.CompilerParams(
            dimension_semantics=("parallel", "arbitrary"), vmem_limit_bytes=VMEM_LIMIT),
        name="experts",
    )(xn, expert_u.astype(BF16), expert_v.astype(BF16).T, s2, a2, cth, a1)


def _final_kernel(h1_ref, ot_ref, p_ref, g_ref, wg_ref, wp_ref, o_ref):
    h2 = h1_ref[...] + jnp.transpose(ot_ref[0])
    hn = _rms(h2, g_ref[...]).astype(BF16)
    gate = jax.nn.sigmoid(jnp.dot(hn, wg_ref[...], preferred_element_type=F32))
    pe = jnp.dot(p_ref[...].astype(BF16), wp_ref[...], preferred_element_type=F32)
    o_ref[...] = h2 + gate * pe


def _final(h1, out_t, p2, g_ple, w_gate, w_proj):
    m = h1.shape[0]
    tm = TM_FINAL
    assert out_t.shape == (m // tm, D_MODEL, tm)
    return pl.pallas_call(
        _final_kernel,
        grid=(m // tm,),
        in_specs=[
            pl.BlockSpec((tm, D_MODEL), lambda i: (i, 0)),
            pl.BlockSpec((1, D_MODEL, tm), lambda i: (i, 0, 0)),
            pl.BlockSpec((tm, PLE_DIM), lambda i: (i, 0)),
            pl.BlockSpec((1, D_MODEL), lambda i: (0, 0)),
            pl.BlockSpec((D_MODEL, D_MODEL), lambda i: (0, 0)),
            pl.BlockSpec((PLE_DIM, D_MODEL), lambda i: (0, 0)),
        ],
        out_specs=pl.BlockSpec((tm, D_MODEL), lambda i: (i, 0)),
        out_shape=jax.ShapeDtypeStruct((m, D_MODEL), F32),
        compiler_params=pltpu.CompilerParams(dimension_semantics=("parallel",), vmem_limit_bytes=VMEM_LIMIT),
        name="final",
    )(h1, out_t, p2, g_ple, w_gate.astype(BF16), w_proj.astype(BF16))


def _layer(h2d, p2d, pos2, batch, seq, norm_mix, w_in, qk_a, qk_b, rel_bias, w_out, norm_ffn,
           peer_query, sub_keys, peer_u, peer_v, norm_ple, ple_gate, ple_proj):
    bias = _relbias_block(rel_bias)
    main, vbt, kiab, wi = _inproj(h2d, pos2, norm_mix[None, :], w_in, qk_a, qk_b)
    out_a = _mixa(main, bias, batch, seq)
    out_b = _dsa(main, vbt, kiab, wi, batch, seq)
    h1, xn = _outproj(out_a, out_b, h2d, w_out, norm_ffn[None, :])
    s2, a2, cth, a1 = _route(xn, peer_query, sub_keys)
    out_t = _experts(xn, peer_u, peer_v, s2, a2, cth, a1)
    return _final(h1, out_t, p2d, norm_ple[None, :], ple_gate, ple_proj)


def kernel(x, p, positions, norm_mix, w_in, qk_norm_a, qk_norm_b, rel_bias, w_out, norm_ffn, peer_query,
           peer_sub_keys, peer_u, peer_v, norm_ple, ple_gate, ple_proj):
    batch, seq, d = x.shape
    assert d == D_MODEL and seq % TM_PROJ == 0 and (batch * seq) % TM_PROJ == 0
    h = x.reshape(batch * seq, d)
    pos2 = positions.reshape(batch * seq, 1).astype(jnp.int32)
    for i in range(p.shape[0]):
        h = _layer(h, p[i].reshape(batch * seq, PLE_DIM), pos2, batch, seq, norm_mix[i], w_in[i],
                   qk_norm_a[i], qk_norm_b[i], rel_bias[i], w_out[i], norm_ffn[i], peer_query[i],
                   peer_sub_keys[i], peer_u[i], peer_v[i], norm_ple[i], ple_gate[i], ple_proj[i])
    return h.reshape(batch, seq, d)
```

```python
import functools

import numpy as np
import jax
import jax.numpy as jnp
from jax import lax
from jax.experimental import pallas as pl
from jax.experimental.pallas import tpu as pltpu

F32 = jnp.float32
BF16 = jnp.bfloat16

D_MODEL = 2048
HEAD_DIM = 128
N_HEADS = 8
WIDTH = N_HEADS * HEAD_DIM
CHUNK = 64
LEFT_CHUNKS = 8
MAX_REL = 128
IDX_HEADS = 16
IDX_DIM = 64
TOPK_MAX = 256
ROPE_THETA = 500000.0
ROPE_FRACTION = 4
PEER_HEADS = 8
PEER_KEYS = 128
PEER_TOPK = 16
PLE_DIM = 256
EPS = 1e-6

LANES = 128
SUBLANES = 8
VMEM_LIMIT = 56 * 1024 * 1024

NEG = -1e30

TM_PROJ = 512
QA_BLOCK = 256
QA_WIN = 768
TQ_DSA = 256
KSUB_DSA = 128
BITS_PER_CHECK = 4
TM_ROUTE = 512
TM_EXP = 512
TE_EXP = 512
TM_FINAL = 256
GATE_ROWS = 32
ANCHOR_EVERY = 2
ANCHOR_SPAN_PCT = 100
TP_EXP = 256

_NT = (((1,), (1,)), ((), ()))


def _nt_dot(a, b):
    return lax.dot_general(a, b, _NT, preferred_element_type=F32)


def _rms(x, g):
    return x * lax.rsqrt(jnp.mean(x * x, axis=-1, keepdims=True) + EPS) * g


def _rope_lane_table(group, half):
    lane = np.arange(LANES)
    m = lane % group
    tab = np.zeros((SUBLANES, LANES), np.float32)
    tab[0] = np.where(m < 2 * half, (m % half) / half, 0.0)
    tab[1] = (m < half)
    tab[2] = (m >= half) & (m < 2 * half)
    tab[3] = (m < 2 * half)
    return tab


def _rope_coeffs(pos_f, tab_ref, out_ref):
    inv_freq = tab_ref[3:4, :] / jnp.power(jnp.float32(ROPE_THETA), tab_ref[0:1, :])
    ang = pos_f * inv_freq
    s = jnp.sin(ang)
    out_ref[0] = jnp.cos(ang)
    out_ref[1] = -s * tab_ref[1:2, :]
    out_ref[2] = s * tab_ref[2:3, :]


def _apply_rope(x, coef_ref, half):
    up = pltpu.roll(x, LANES - half, 1)
    dn = pltpu.roll(x, half, 1)
    return x * coef_ref[0] + up * coef_ref[1] + dn * coef_ref[2]


def _relbias_kernel(u_ref, o_ref):
    x = jnp.broadcast_to(u_ref[0, 0:1, :], (QA_BLOCK, 1024))
    y = pltpu.roll(x, 0, 1, stride=1, stride_axis=0)[:, :QA_WIN]
    qc = lax.broadcasted_iota(jnp.int32, (QA_BLOCK, QA_WIN), 0) // CHUNK
    kc = lax.broadcasted_iota(jnp.int32, (QA_BLOCK, QA_WIN), 1) // CHUNK
    band = (kc >= qc) & (kc <= qc + LEFT_CHUNKS)
    o_ref[0] = jnp.where(band, y, NEG)


def _relbias_block(rel_bias):
    p = np.arange(1024)
    off = np.where(p < QA_WIN, p, p - 1024)
    idx = np.clip(LEFT_CHUNKS * CHUNK - off, -MAX_REL, MAX_REL) + MAX_REL
    u = jnp.take(rel_bias.astype(F32), jnp.asarray(idx, jnp.int32), axis=1)
    u = jnp.broadcast_to(u[:, None, :], (N_HEADS, SUBLANES, 1024))
    return pl.pallas_call(
        _relbias_kernel,
        grid=(N_HEADS,),
        in_specs=[pl.BlockSpec((1, SUBLANES, 1024), lambda h: (h, 0, 0))],
        out_specs=pl.BlockSpec((1, QA_BLOCK, QA_WIN), lambda h: (h, 0, 0)),
        out_shape=jax.ShapeDtypeStruct((N_HEADS, QA_BLOCK, QA_WIN), F32),
        name="relbias",
    )(u)


R_QB, R_VA, R_KB, R_KA, R_QI, R_QA = range(6)


def _inproj_kernel(x_ref, g_ref, w0_ref, w1_ref, wvt_ref, wtail_ref, qkg_ref, pos_ref, tab128_ref, tab64_ref,
                   main_ref, vbt_ref, kiab_ref, wi_ref,
                   hn_ref, acc0_ref, acc1_ref, rope128_ref, rope64_ref):
    j = pl.program_id(1)
    tm = x_ref.shape[0]

    @pl.when(j == 0)
    def _():
        hn_ref[...] = _rms(x_ref[...], g_ref[...]).astype(BF16)
        pos_f = pos_ref[...].astype(F32)
        _rope_coeffs(pos_f, tab128_ref, rope128_ref)
        _rope_coeffs(pos_f, tab64_ref, rope64_ref)
        vt = _nt_dot(wvt_ref[...], hn_ref[...]).astype(BF16)
        for t in range(tm // TQ_DSA):
            vbt_ref[t] = vt[:, t * TQ_DSA:(t + 1) * TQ_DSA]
        tail = jnp.dot(hn_ref[...], wtail_ref[...], preferred_element_type=F32)
        kk = _apply_rope(tail[:, :LANES], rope64_ref, IDX_DIM // ROPE_FRACTION // 2)
        lane = lax.broadcasted_iota(jnp.int32, (tm, LANES), 1)
        kiab_ref[:, :LANES] = jnp.where(lane < IDX_DIM, kk, 0.0).astype(BF16)
        kiab_ref[:, LANES:] = jnp.where(lane >= IDX_DIM, kk, 0.0).astype(BF16)
        wi_ref[...] = tail[:, LANES:]

    half_b = HEAD_DIM // ROPE_FRACTION // 2
    half_i = IDX_DIM // ROPE_FRACTION // 2
    epilogues = {
        R_QA: lambda a: _rms(a, qkg_ref[0:1, :]),
        R_KA: lambda a: _rms(a, qkg_ref[1:2, :]),
        R_VA: lambda a: a,
        R_QB: lambda a: _apply_rope(_rms(a, qkg_ref[2:3, :]), rope128_ref, half_b),
        R_KB: lambda a: _apply_rope(_rms(a, qkg_ref[3:4, :]), rope128_ref, half_b),
        R_QI: lambda a: _apply_rope(a, rope64_ref, half_i),
    }

    def project_pair(first):
        accs = (acc0_ref, acc1_ref)
        for r, w_ref in enumerate((w0_ref, w1_ref)):
            accs[r][...] = jnp.dot(hn_ref[...], w_ref[...], preferred_element_type=F32)
        for r in range(2):
            fn = epilogues[first + r]
            for h in range(N_HEADS):
                sl = slice(h * HEAD_DIM, (h + 1) * HEAD_DIM)
                main_ref[:, r * WIDTH + h * HEAD_DIM:r * WIDTH + (h + 1) * HEAD_DIM] = fn(accs[r][:, sl]).astype(BF16)

    for pair in range(3):
        pl.when(j == pair)(functools.partial(project_pair, 2 * pair))


def _inproj(x2, pos2, g_mix, w_in, qk_a, qk_b):
    m = x2.shape[0]
    tm = TM_PROJ
    w = w_in.astype(BF16)
    o = [0, WIDTH, 2 * WIDTH, 3 * WIDTH, 4 * WIDTH, 5 * WIDTH, 6 * WIDTH, 7 * WIDTH]
    src = {R_QA: 0, R_KA: 1, R_VA: 2, R_QB: 3, R_KB: 4, R_QI: 6}

    def w_spec(r):
        blocks = [src[2 * pair + r] for pair in range(3)]
        return pl.BlockSpec((D_MODEL, WIDTH),
                            lambda i, j: (0, jnp.where(j == 0, blocks[0], jnp.where(j == 1, blocks[1], blocks[2]))))

    w_vt = w[:, o[5]:o[6]].T
    w_ki = w[:, o[7]:o[7] + IDX_DIM]
    w_wi = w[:, o[7] + IDX_DIM:o[7] + IDX_DIM + IDX_HEADS]
    w_tail = jnp.concatenate([w_ki, w_ki, w_wi, jnp.zeros((D_MODEL, LANES - IDX_HEADS), BF16)], axis=1)
    qkg = jnp.concatenate([qk_a, qk_b, jnp.zeros((4, HEAD_DIM), F32)], axis=0).astype(F32)
    tab128 = jnp.asarray(_rope_lane_table(HEAD_DIM, HEAD_DIM // ROPE_FRACTION // 2))
    tab64 = jnp.asarray(_rope_lane_table(IDX_DIM, IDX_DIM // ROPE_FRACTION // 2))
    res = lambda shape: pl.BlockSpec(shape, lambda i, j: (0,) * len(shape))
    return pl.pallas_call(
        _inproj_kernel,
        grid=(m // tm, 3),
        in_specs=[
            pl.BlockSpec((tm, D_MODEL), lambda i, j: (i, 0)),
            res((1, D_MODEL)),
            w_spec(0),
            w_spec(1),
            res((WIDTH, D_MODEL)),
            res((D_MODEL, 2 * LANES)),
            res((SUBLANES, HEAD_DIM)),
            pl.BlockSpec((tm, 1), lambda i, j: (i, 0)),
            res((SUBLANES, LANES)),
            res((SUBLANES, LANES)),
        ],
        out_specs=[
            pl.BlockSpec((tm, 2 * WIDTH), lambda i, j: (i, j)),
            pl.BlockSpec((tm // TQ_DSA, WIDTH, TQ_DSA), lambda i, j: (i, 0, 0)),
            pl.BlockSpec((tm, 2 * LANES), lambda i, j: (i, 0)),
            pl.BlockSpec((tm, LANES), lambda i, j: (i, 0)),
        ],
        out_shape=[
            jax.ShapeDtypeStruct((m, 6 * WIDTH), BF16),
            jax.ShapeDtypeStruct((m // TQ_DSA, WIDTH, TQ_DSA), BF16),
            jax.ShapeDtypeStruct((m, 2 * LANES), BF16),
            jax.ShapeDtypeStruct((m, LANES), F32),
        ],
        scratch_shapes=[
            pltpu.VMEM((tm, D_MODEL), BF16),
            pltpu.VMEM((tm, WIDTH), F32),
            pltpu.VMEM((tm, WIDTH), F32),
            pltpu.VMEM((3, tm, LANES), F32),
            pltpu.VMEM((3, tm, LANES), F32),
        ],
        compiler_params=pltpu.CompilerParams(
            dimension_semantics=("parallel", "arbitrary"), vmem_limit_bytes=VMEM_LIMIT),
        name="inproj",
    )(x2, g_mix, w, w, w_vt, w_tail, qkg, pos2, tab128, tab64)


def _mixa_kernel(q_ref, k0_ref, k1_ref, k2_ref, v0_ref, v1_ref, v2_ref, bias_ref, o_ref):
    qb = pl.program_id(1)
    scale = HEAD_DIM ** -0.5
    k_refs = (k0_ref, k1_ref, k2_ref)
    v_refs = (v0_ref, v1_ref, v2_ref)
    pen = (jnp.where(qb >= 2, 0.0, NEG).astype(F32), jnp.where(qb >= 1, 0.0, NEG).astype(F32), None)
    for h in range(N_HEADS):
        sl = slice(h * HEAD_DIM, (h + 1) * HEAD_DIM)
        q = q_ref[:, sl]
        s = []
        for m in range(3):
            sm = _nt_dot(q, k_refs[m][:, sl]) * scale + bias_ref[h, :, m * QA_BLOCK:(m + 1) * QA_BLOCK]
            if pen[m] is not None:
                sm = sm + pen[m]
            s.append(sm)
        mx = jnp.maximum(jnp.maximum(jnp.max(s[0], axis=-1, keepdims=True), jnp.max(s[1], axis=-1, keepdims=True)),
                         jnp.max(s[2], axis=-1, keepdims=True))
        p = [jnp.exp(sm - mx) for sm in s]
        den = (jnp.sum(p[0], axis=-1, keepdims=True) + jnp.sum(p[1], axis=-1, keepdims=True)
               + jnp.sum(p[2], axis=-1, keepdims=True))
        o = jnp.dot(p[0].astype(BF16), v_refs[0][:, sl], preferred_element_type=F32)
        o = o + jnp.dot(p[1].astype(BF16), v_refs[1][:, sl], preferred_element_type=F32)
        o = o + jnp.dot(p[2].astype(BF16), v_refs[2][:, sl], preferred_element_type=F32)
        o_ref[:, sl] = (o / den).astype(BF16)


def _mixa(main, bias, batch, seq):
    m = main.shape[0]
    nqb = seq // QA_BLOCK

    def kv_spec(region, back):
        return pl.BlockSpec((QA_BLOCK, WIDTH),
                            lambda b, i: (b * nqb + jnp.maximum(i - back, 0), region))

    return pl.pallas_call(
        _mixa_kernel,
        grid=(batch, nqb),
        in_specs=[
            pl.BlockSpec((QA_BLOCK, WIDTH), lambda b, i: (b * nqb + i, R_QA)),
            kv_spec(R_KA, 2), kv_spec(R_KA, 1), kv_spec(R_KA, 0),
            kv_spec(R_VA, 2), kv_spec(R_VA, 1), kv_spec(R_VA, 0),
            pl.BlockSpec((N_HEADS, QA_BLOCK, QA_WIN), lambda b, i: (0, 0, 0)),
        ],
        out_specs=pl.BlockSpec((QA_BLOCK, WIDTH), lambda b, i: (b * nqb + i, 0)),
        out_shape=jax.ShapeDtypeStruct((m, WIDTH), BF16),
        compiler_params=pltpu.CompilerParams(
            dimension_semantics=("parallel", "parallel"), vmem_limit_bytes=VMEM_LIMIT),
        name="mixa",
    )(main, main, main, main, main, main, main, bias)


def _key_to_f32(t):
    return pltpu.bitcast(t ^ ((t >> 31) & jnp.int32(0x7FFFFFFF)), F32)


def _dsa_kernel(topk, qb_ref, kb_ref, vbt_ref, qi_ref, kiab_ref, wi_ref, o_ref,
                sc_ref, bias_ref, acc_ref, m_ref, l_ref):
    i = pl.program_id(1)
    tq = TQ_DSA
    nkt = i + 1
    c2 = HEAD_DIM ** -0.5 * 1.4426950408889634

    w_t = jnp.transpose(wi_ref[...]) * (IDX_HEADS ** -0.5 * IDX_DIM ** -0.5)
    qchunk = (i * tq + lax.broadcasted_iota(jnp.int32, (1, tq), 1)) // CHUNK
    row = lax.broadcasted_iota(jnp.int32, (tq, tq), 0)

    def admissible(kt):
        return (kt * tq + row) // CHUNK <= qchunk

    def idx_body(kt, carry):
        r0 = pl.multiple_of(kt * tq, tq)
        ka = kiab_ref[pl.ds(r0, tq), :LANES]
        kb = kiab_ref[pl.ds(r0, tq), LANES:]
        acc = jnp.zeros((tq, tq), F32)
        for p in range(IDX_HEADS // 2):
            qp = qi_ref[:, p * LANES:(p + 1) * LANES]
            acc = acc + w_t[2 * p:2 * p + 1, :] * jnp.maximum(_nt_dot(ka, qp), 0.0)
            acc = acc + w_t[2 * p + 1:2 * p + 2, :] * jnp.maximum(_nt_dot(kb, qp), 0.0)
        sc_ref[pl.ds(r0, tq), :] = jnp.where(admissible(kt), acc, -jnp.inf)
        return carry

    lax.fori_loop(0, nkt, idx_body, 0)

    n_adm = (qchunk + 1) * CHUNK
    need = n_adm > topk

    def count_ge(c_f):
        def cnt_body(kt, c):
            r0 = pl.multiple_of(kt * tq, tq)
            ge = jnp.where(sc_ref[pl.ds(r0, tq), :] >= c_f, 1, 0).astype(jnp.int32)
            return c + jnp.sum(ge.reshape(tq // SUBLANES, SUBLANES, tq), axis=0)

        c8 = lax.fori_loop(0, nkt, cnt_body, jnp.zeros((SUBLANES, tq), jnp.int32))
        return jnp.sum(c8, axis=0, keepdims=True)

    def bit_cond(carry):
        b, _, cnt = carry
        todo = jnp.max(jnp.where(need & (cnt != topk), 1, 0))
        return (b >= 0) & (todo > 0)

    def bit_body(carry):
        b, thr, cnt = carry
        for _ in range(BITS_PER_CHECK):
            cand = thr + lax.shift_left(jnp.int32(1), b)
            n = count_ge(_key_to_f32(cand))
            ok = n >= topk
            b, thr, cnt = b - 1, jnp.where(ok, cand, thr), jnp.where(ok, n, cnt)
        return b, thr, cnt

    init = (jnp.int32(31), jnp.full((1, tq), jnp.iinfo(jnp.int32).min, jnp.int32),
            jnp.full((1, tq), jnp.iinfo(jnp.int32).max, jnp.int32))
    _, thr, _ = lax.while_loop(bit_cond, bit_body, init)
    thr_f = _key_to_f32(thr)

    def mask_body(kt, carry):
        r0 = pl.multiple_of(kt * tq, tq)
        sel = admissible(kt) & ((sc_ref[pl.ds(r0, tq), :] >= thr_f) | jnp.logical_not(need))
        bias_ref[pl.ds(r0, tq), :] = jnp.where(sel, 0.0, NEG).astype(F32)
        return carry

    lax.fori_loop(0, nkt, mask_body, 0)

    acc_ref[...] = jnp.zeros_like(acc_ref)
    m_ref[...] = jnp.full(m_ref.shape, NEG, F32)
    l_ref[...] = jnp.zeros_like(l_ref)

    def att_body(kt, carry):
        for half in range(tq // KSUB_DSA):
            r0 = pl.multiple_of(kt * tq + half * KSUB_DSA, KSUB_DSA)
            ks = slice(half * KSUB_DSA, (half + 1) * KSUB_DSA)
            for h in range(N_HEADS):
                sl = slice(h * HEAD_DIM, (h + 1) * HEAD_DIM)
                s = _nt_dot(kb_ref[pl.ds(r0, KSUB_DSA), sl], qb_ref[:, sl]) + bias_ref[pl.ds(r0, KSUB_DSA), :]
                m_old = m_ref[h:h + 1, :]
                m_new = jnp.maximum(m_old, jnp.max(s, axis=0, keepdims=True))
                alpha = jnp.exp2((m_old - m_new) * c2)
                p = jnp.exp2((s - m_new) * c2)
                l_ref[h:h + 1, :] = alpha * l_ref[h:h + 1, :] + jnp.sum(p, axis=0, keepdims=True)
                m_ref[h:h + 1, :] = m_new
                pv = jnp.dot(vbt_ref[kt, sl, ks], p.astype(BF16), preferred_element_type=F32)
                acc_ref[h] = alpha * acc_ref[h] + pv
        return carry

    lax.fori_loop(0, nkt, att_body, 0)
    for h in range(N_HEADS):
        sl = slice(h * HEAD_DIM, (h + 1) * HEAD_DIM)
        o_ref[:, sl] = jnp.transpose(acc_ref[h] / l_ref[h:h + 1, :]).astype(BF16)


def _dsa(main, vbt, kiab, wi, batch, seq):
    m = main.shape[0]
    tq = TQ_DSA
    nq = seq // tq
    topk = min(TOPK_MAX, seq // 4)
    return pl.pallas_call(
        functools.partial(_dsa_kernel, topk),
        grid=(batch, nq),
        in_specs=[
            pl.BlockSpec((tq, WIDTH), lambda b, i: (b * nq + i, R_QB)),
            pl.BlockSpec((seq, WIDTH), lambda b, i: (b, R_KB)),
            pl.BlockSpec((nq, WIDTH, tq), lambda b, i: (b, 0, 0)),
            pl.BlockSpec((tq, WIDTH), lambda b, i: (b * nq + i, R_QI)),
            pl.BlockSpec((seq, 2 * LANES), lambda b, i: (b, 0)),
            pl.BlockSpec((tq, LANES), lambda b, i: (b * nq + i, 0)),
        ],
        out_specs=pl.BlockSpec((tq, WIDTH), lambda b, i: (b * nq + i, 0)),
        out_shape=jax.ShapeDtypeStruct((m, WIDTH), BF16),
        scratch_shapes=[pltpu.VMEM((seq, tq), F32), pltpu.VMEM((seq, tq), F32),
                        pltpu.VMEM((N_HEADS, HEAD_DIM, tq), F32),
                        pltpu.VMEM((N_HEADS, tq), F32), pltpu.VMEM((N_HEADS, tq), F32)],
        compiler_params=pltpu.CompilerParams(
            dimension_semantics=("parallel", "arbitrary"), vmem_limit_bytes=VMEM_LIMIT),
        name="dsa",
    )(main, main, vbt, main, kiab, wi)


def _outproj_kernel(oa_ref, ob_ref, x_ref, wo_ref, g_ref, h1_ref, xn_ref):
    acc = jnp.dot(oa_ref[...], wo_ref[:WIDTH, :], preferred_element_type=F32)
    acc = acc + jnp.dot(ob_ref[...], wo_ref[WIDTH:, :], preferred_element_type=F32)
    h1 = x_ref[...] + acc
    h1_ref[...] = h1
    xn_ref[...] = _rms(h1, g_ref[...]).astype(BF16)


def _outproj(out_a, out_b, x2, w_out, g_ffn):
    m = x2.shape[0]
    tm = TM_PROJ
    return pl.pallas_call(
        _outproj_kernel,
        grid=(m // tm,),
        in_specs=[
            pl.BlockSpec((tm, WIDTH), lambda i: (i, 0)),
            pl.BlockSpec((tm, WIDTH), lambda i: (i, 0)),
            pl.BlockSpec((tm, D_MODEL), lambda i: (i, 0)),
            pl.BlockSpec((2 * WIDTH, D_MODEL), lambda i: (0, 0)),
            pl.BlockSpec((1, D_MODEL), lambda i: (0, 0)),
        ],
        out_specs=[pl.BlockSpec((tm, D_MODEL), lambda i: (i, 0)), pl.BlockSpec((tm, D_MODEL), lambda i: (i, 0))],
        out_shape=[jax.ShapeDtypeStruct((m, D_MODEL), F32), jax.ShapeDtypeStruct((m, D_MODEL), BF16)],
        compiler_params=pltpu.CompilerParams(dimension_semantics=("parallel",), vmem_limit_bytes=VMEM_LIMIT),
        name="outproj",
    )(out_a, out_b, x2, w_out.astype(BF16), g_ffn)


def _merge_exchange_pairs(n):
    t = (n - 1).bit_length()
    pairs = []
    p = 1 << (t - 1)
    while p > 0:
        q, r, d = 1 << (t - 1), 0, p
        while True:
            pairs.extend((i, i + d) for i in range(n - d) if (i & p) == r)
            if q == p:
                break
            d, q, r = q - p, q >> 1, p
        p >>= 1
    return pairs


_SORT16 = _merge_exchange_pairs(PEER_TOPK)


def _tmax(a, b):
    if a is None:
        return b
    if b is None:
        return a
    return jnp.maximum(a, b)


def _cmp_exchange(x, i, j):
    a, b = x[i], x[j]
    if b is None:
        return
    if a is None:
        x[i], x[j] = b, None
        return
    x[i], x[j] = jnp.maximum(a, b), jnp.minimum(a, b)


def _sort16_desc(x):
    for i, j in _SORT16:
        _cmp_exchange(x, i, j)


def _bitonic16_desc(x):
    for d in (8, 4, 2, 1):
        for i in range(PEER_TOPK):
            if not i & d:
                _cmp_exchange(x, i, i + d)


def _merge_sublanes(x, shift, sort=True):
    y = [None if v is None else pltpu.roll(v, shift, 0) for v in x]
    out = [_tmax(x[i], y[PEER_TOPK - 1 - i]) for i in range(PEER_TOPK)]
    if sort:
        _bitonic16_desc(out)
    return out


def _top16(tiles):
    x = list(tiles)
    _sort16_desc(x)
    for shift in (4, 2, 1):
        x = _merge_sublanes(x, shift)
    return x


def _pair_threshold(v1, v2):
    sub = lax.broadcasted_iota(jnp.int32, v1[0].shape, 0)

    def diag(vals):
        d = vals[0]
        for s in range(1, SUBLANES):
            d = jnp.where(sub == s, vals[s], d)
        return d

    d1a, d1b = diag(v1[:8]), diag(v1[8:])
    d2a, d2b = diag(v2[:8]), diag(v2[8:])
    cand = [d1a + v2[0], d1b + v2[0], d1a + v2[1], d1a + v2[2], d1a + v2[3], v1[0] + d2b]
    cand += [jnp.where(sub < 4, -jnp.inf, v1[r] + d2a) for r in range(3)]
    x = cand + [None] * (PEER_TOPK - len(cand))
    _sort16_desc(x)
    x = _merge_sublanes(x, 4)
    x = _merge_sublanes(x, 2)
    x = _merge_sublanes(x, 1, sort=False)
    tau = x[0]
    for v in x[1:]:
        tau = jnp.minimum(tau, v)
    return tau


def _sublane_total(x):
    for shift in (4, 2, 1):
        x = x + pltpu.roll(x, shift, 0)
    return x


def _route_kernel(xn_ref, wq_ref, sk_ref, s2_ref, a2_ref, c_ref, a1_ref, qv_ref):
    tm = xn_ref.shape[0]
    n_t = PEER_KEYS // SUBLANES
    qv = jnp.dot(xn_ref[...], wq_ref[...], preferred_element_type=F32)
    for c in range(2 * PEER_HEADS):
        qv_ref[c] = qv[:, c * PEER_KEYS:(c + 1) * PEER_KEYS].astype(BF16)

    def chunk(h, lc):
        t0 = pl.multiple_of(lc * LANES, LANES)
        s1 = _nt_dot(sk_ref[0], qv_ref[2 * h, pl.ds(t0, LANES), :])
        s2 = _nt_dot(sk_ref[1], qv_ref[2 * h + 1, pl.ds(t0, LANES), :])
        s1 = [s1[j * SUBLANES:(j + 1) * SUBLANES, :] for j in range(n_t)]
        s2 = [s2[j * SUBLANES:(j + 1) * SUBLANES, :] for j in range(n_t)]
        v1 = _top16(s1)
        v2 = _top16(s2)
        tau = _pair_threshold(v1, v2)
        pre = []
        for q in range(PEER_TOPK):
            e = jnp.exp(v2[q] - v2[0])
            pre.append(e if q == 0 else pre[-1] + e)
        half_q = PEER_TOPK // 2
        c_top = jnp.full(v1[0].shape, jnp.inf, F32)
        m_top = jnp.zeros(v1[0].shape, F32)
        for q in range(half_q, PEER_TOPK):
            ok = v1[0] + v2[q] >= tau
            c_top = jnp.where(ok, v2[q], c_top)
            m_top = jnp.where(ok, pre[q], m_top)
        deep = c_top < jnp.inf
        a1, cth, z = [], [], None
        for j in range(n_t):
            c_j = jnp.full(s1[j].shape, jnp.inf, F32)
            m_j = jnp.zeros(s1[j].shape, F32)
            for q in range(half_q):
                ok = s1[j] + v2[q] >= tau
                c_j = jnp.where(ok, v2[q], c_j)
                m_j = jnp.where(ok, pre[q], m_j)
            top = (s1[j] == v1[0]) & deep
            c_j = jnp.where(top, c_top, c_j)
            m_j = jnp.where(top, m_top, m_j)
            a_j = jnp.exp(s1[j] - v1[0])
            a1.append(a_j)
            cth.append(c_j)
            z = a_j * m_j if z is None else z + a_j * m_j
        z = _sublane_total(z)
        for j in range(n_t):
            rs = slice(j * SUBLANES, (j + 1) * SUBLANES)
            s2_ref[h, lc, rs, :] = s2[j]
            a2_ref[h, lc, rs, :] = jnp.exp(s2[j] - v2[0])
            c_ref[h, lc, rs, :] = cth[j]
            a1_ref[h, lc, rs, :] = a1[j] / z

    def head_body(h, carry):
        def chunk_body(lc, c):
            chunk(h, lc)
            return c

        return lax.fori_loop(0, tm // LANES, chunk_body, carry)

    lax.fori_loop(0, PEER_HEADS, head_body, 0)


def _route(xn, w_query, sub_keys):
    m = xn.shape[0]
    tm = TM_ROUTE
    spec = pl.BlockSpec((PEER_HEADS, tm // LANES, PEER_KEYS, LANES), lambda i: (0, i, 0, 0))
    shp = jax.ShapeDtypeStruct((PEER_HEADS, m // LANES, PEER_KEYS, LANES), F32)
    return pl.pallas_call(
        _route_kernel,
        grid=(m // tm,),
        in_specs=[
            pl.BlockSpec((tm, D_MODEL), lambda i: (i, 0)),
            pl.BlockSpec((D_MODEL, 2 * PEER_HEADS * PEER_KEYS), lambda i: (0, 0)),
            pl.BlockSpec((2, PEER_KEYS, PEER_KEYS), lambda i: (0, 0, 0)),
        ],
        out_specs=[spec, spec, spec, spec],
        out_shape=[shp, shp, shp, shp],
        scratch_shapes=[pltpu.VMEM((2 * PEER_HEADS, tm, PEER_KEYS), BF16)],
        compiler_params=pltpu.CompilerParams(dimension_semantics=("parallel",), vmem_limit_bytes=VMEM_LIMIT),
        name="route",
    )(xn, w_query.astype(BF16), sub_keys.astype(BF16))


def _zero_row_from(x):
    bits = pltpu.bitcast(x, jnp.int32)
    z = lax.shift_right_logical(lax.shift_right_logical(bits, 16), 16)
    return z[0:1, :].astype(F32)


def _gate_piece(h_ref, coef_ref, p, e1_base, s2_ref, a2_ref, c_ref, a1_ref, anchors):
    n_e1 = TE_EXP // PEER_KEYS
    unit = 0
    for c in range(TP_EXP // LANES):
        lc = p * (TP_EXP // LANES) + c
        ls = slice(c * LANES, (c + 1) * LANES)
        for r0 in range(0, PEER_KEYS, GATE_ROWS):
            kr = slice(r0, r0 + GATE_ROWS)
            for e0 in range(0, n_e1, 2):
                gates = [jnp.zeros((GATE_ROWS, LANES), F32), jnp.zeros((GATE_ROWS, LANES), F32)]
                for h in range(PEER_HEADS):
                    anchor = anchors[(unit * PEER_HEADS + h) // ANCHOR_EVERY]
                    s2 = s2_ref[h, lc, kr, :]
                    a2 = a2_ref[h, lc, kr, :]
                    for k in range(2):
                        e1 = e1_base + e0 + k
                        cth = c_ref[h, lc, pl.ds(e1, 1), :] + anchor
                        a1 = a1_ref[h, lc, pl.ds(e1, 1), :]
                        gates[k] = gates[k] + jnp.where(s2 >= cth, a2, 0.0) * a1
                for k in range(2):
                    rs = slice((e0 + k) * PEER_KEYS + r0, (e0 + k) * PEER_KEYS + r0 + GATE_ROWS)
                    act = jax.nn.gelu(h_ref[p, rs, ls])
                    coef_ref[p, rs, ls] = (gates[k] * act).astype(BF16)
                unit += 1


def _experts_kernel(n_steps, xn_ref, u_ref, vt_ref, s2_ref, a2_ref, c_ref, a1_ref, o_ref,
                    h0_ref, h1_ref, c0_ref, c1_ref):
    g = pl.program_id(1)
    te = TE_EXP
    n_e1 = te // PEER_KEYS
    last_e1 = (2 * n_steps - 1) * n_e1
    n_pieces = xn_ref.shape[0] // TP_EXP
    n_anchor = (TP_EXP // LANES) * (PEER_KEYS // GATE_ROWS) * (n_e1 // 2) * PEER_HEADS // ANCHOR_EVERY

    @pl.when(g == 0)
    def _():
        o_ref[...] = jnp.zeros_like(o_ref)

    gates = (s2_ref, a2_ref, c_ref, a1_ref)

    def half_step(u_rows, v_cols, hb_ref, hc_ref, cc_ref, ca_ref, e1_base, do_b, do_c, do_a):
        def body(p, carry):
            t0 = pl.multiple_of(p * TP_EXP, TP_EXP)
            if do_b:
                hb_ref[p] = _nt_dot(u_ref[u_rows, :], xn_ref[pl.ds(t0, TP_EXP), :])
            if do_a:
                res = jnp.dot(vt_ref[:, v_cols], ca_ref[p], preferred_element_type=F32)
                o_ref[p] += res
            if do_c:
                if do_a:
                    step = (D_MODEL * ANCHOR_SPAN_PCT // 100) // n_anchor // SUBLANES * SUBLANES
                    anchors = [_zero_row_from(res[k * step:k * step + SUBLANES, 0:LANES]) for k in range(n_anchor)]
                else:
                    anchors = [jnp.zeros((1, LANES), F32)] * n_anchor
                _gate_piece(hc_ref, cc_ref, p, e1_base, *gates, anchors)
            return carry

        lax.fori_loop(0, n_pieces, body, 0)

    def step(first, second):
        half_step(slice(0, te), slice(0, te), h0_ref, h1_ref, c1_ref, c0_ref,
                  jnp.clip((2 * g - 1) * n_e1, 0, last_e1), *first)
        half_step(slice(te, 2 * te), slice(te, 2 * te), h1_ref, h0_ref, c0_ref, c1_ref,
                  jnp.minimum(2 * g * n_e1, last_e1), *second)

    pl.when(g == 0)(functools.partial(step, (True, False, False), (True, True, False)))
    pl.when((g > 0) & (g < n_steps))(functools.partial(step, (True, True, True), (True, True, True)))
    pl.when(g == n_steps)(functools.partial(step, (False, True, True), (False, False, True)))


def _experts(xn, expert_u, expert_v, s2, a2, cth, a1):
    m = xn.shape[0]
    tm, te, tp = TM_EXP, TE_EXP, TP_EXP
    n_exp = expert_u.shape[0]
    n_steps = n_exp // (2 * te)
    gspec = pl.BlockSpec((PEER_HEADS, tm // LANES, PEER_KEYS, LANES), lambda i, g: (0, i, 0, 0))
    return pl.pallas_call(
        functools.partial(_experts_kernel, n_steps),
        grid=(m // tm, n_steps + 1),
        in_specs=[
            pl.BlockSpec((tm, D_MODEL), lambda i, g: (i, 0)),
            pl.BlockSpec((2 * te, D_MODEL), lambda i, g: (jnp.minimum(g, n_steps - 1), 0)),
            pl.BlockSpec((D_MODEL, 2 * te), lambda i, g: (0, jnp.maximum(g - 1, 0))),
            gspec, gspec, gspec, gspec,
        ],
        out_specs=pl.BlockSpec((tm // tp, D_MODEL, tp), lambda i, g: (i, 0, 0)),
        out_shape=jax.ShapeDtypeStruct((m // tp, D_MODEL, tp), F32),
        scratch_shapes=[pltpu.VMEM((tm // tp, te, tp), F32), pltpu.VMEM((tm // tp, te, tp), F32),
                        pltpu.VMEM((tm // tp, te, tp), BF16), pltpu.VMEM((tm // tp, te, tp), BF16)],
        compiler_params=pltpu.CompilerParams(
            dimension_semantics=("parallel", "arbitrary"), vmem_limit_bytes=VMEM_LIMIT),
        name="experts",
    )(xn, expert_u.astype(BF16), expert_v.astype(BF16).T, s2, a2, cth, a1)


def _final_kernel(h1_ref, ot_ref, p_ref, g_ref, wg_ref, wp_ref, o_ref):
    h2 = h1_ref[...] + jnp.transpose(ot_ref[0])
    hn = _rms(h2, g_ref[...]).astype(BF16)
    gate = jax.nn.sigmoid(jnp.dot(hn, wg_ref[...], preferred_element_type=F32))
    pe = jnp.dot(p_ref[...].astype(BF16), wp_ref[...], preferred_element_type=F32)
    o_ref[...] = h2 + gate * pe


def _final(h1, out_t, p2, g_ple, w_gate, w_proj):
    m = h1.shape[0]
    tm = TM_FINAL
    assert out_t.shape == (m // tm, D_MODEL, tm)
    return pl.pallas_call(
        _final_kernel,
        grid=(m // tm,),
        in_specs=[
            pl.BlockSpec((tm, D_MODEL), lambda i: (i, 0)),
            pl.BlockSpec((1, D_MODEL, tm), lambda i: (i, 0, 0)),
            pl.BlockSpec((tm, PLE_DIM), lambda i: (i, 0)),
            pl.BlockSpec((1, D_MODEL), lambda i: (0, 0)),
            pl.BlockSpec((D_MODEL, D_MODEL), lambda i: (0, 0)),
            pl.BlockSpec((PLE_DIM, D_MODEL), lambda i: (0, 0)),
        ],
        out_specs=pl.BlockSpec((tm, D_MODEL), lambda i: (i, 0)),
        out_shape=jax.ShapeDtypeStruct((m, D_MODEL), F32),
        compiler_params=pltpu.CompilerParams(dimension_semantics=("parallel",), vmem_limit_bytes=VMEM_LIMIT),
        name="final",
    )(h1, out_t, p2, g_ple, w_gate.astype(BF16), w_proj.astype(BF16))


def _layer(h2d, p2d, pos2, batch, seq, norm_mix, w_in, qk_a, qk_b, rel_bias, w_out, norm_ffn,
           peer_query, sub_keys, peer_u, peer_v, norm_ple, ple_gate, ple_proj):
    bias = _relbias_block(rel_bias)
    main, vbt, kiab, wi = _inproj(h2d, pos2, norm_mix[None, :], w_in, qk_a, qk_b)
    out_a = _mixa(main, bias, batch, seq)
    out_b = _dsa(main, vbt, kiab, wi, batch, seq)
    h1, xn = _outproj(out_a, out_b, h2d, w_out, norm_ffn[None, :])
    s2, a2, cth, a1 = _route(xn, peer_query, sub_keys)
    out_t = _experts(xn, peer_u, peer_v, s2, a2, cth, a1)
    return _final(h1, out_t, p2d, norm_ple[None, :], ple_gate, ple_proj)


def kernel(x, p, positions, norm_mix, w_in, qk_norm_a, qk_norm_b, rel_bias, w_out, norm_ffn, peer_query,
           peer_sub_keys, peer_u, peer_v, norm_ple, ple_gate, ple_proj):
    batch, seq, d = x.shape
    assert d == D_MODEL and seq % TM_PROJ == 0 and (batch * seq) % TM_PROJ == 0
    h = x.reshape(batch * seq, d)
    pos2 = positions.reshape(batch * seq, 1).astype(jnp.int32)
    for i in range(p.shape[0]):
        h = _layer(h, p[i].reshape(batch * seq, PLE_DIM), pos2, batch, seq, norm_mix[i], w_in[i],
                   qk_norm_a[i], qk_norm_b[i], rel_bias[i], w_out[i], norm_ffn[i], peer_query[i],
                   peer_sub_keys[i], peer_u[i], peer_v[i], norm_ple[i], ple_gate[i], ple_proj[i])
    return h.reshape(batch, seq, d)
```

```python
import functools

import numpy as np
import jax
import jax.numpy as jnp
from jax import lax
from jax.experimental import pallas as pl
from jax.experimental.pallas import tpu as pltpu

F32 = jnp.float32
BF16 = jnp.bfloat16

D_MODEL = 2048
HEAD_DIM = 128
N_HEADS = 8
WIDTH = N_HEADS * HEAD_DIM
CHUNK = 64
LEFT_CHUNKS = 8
MAX_REL = 128
IDX_HEADS = 16
IDX_DIM = 64
TOPK_MAX = 256
ROPE_THETA = 500000.0
ROPE_FRACTION = 4
PEER_HEADS = 8
PEER_KEYS = 128
PEER_TOPK = 16
PLE_DIM = 256
EPS = 1e-6

LANES = 128
SUBLANES = 8
VMEM_LIMIT = 56 * 1024 * 1024

NEG = -1e30

TM_PROJ = 512
QA_BLOCK = 256
QA_WIN = 768
TQ_DSA = 256
KSUB_DSA = 128
BITS_PER_CHECK = 4
UNCHECKED_BITS = 20
TM_ROUTE = 512
TM_EXP = 512
TE_EXP = 512
TM_FINAL = 256
GATE_ROWS = 32
ANCHOR_EVERY = 2
ANCHOR_SPAN_PCT = 100
TP_EXP = 256

_NT = (((1,), (1,)), ((), ()))


def _nt_dot(a, b):
    return lax.dot_general(a, b, _NT, preferred_element_type=F32)


def _rms(x, g):
    return x * lax.rsqrt(jnp.mean(x * x, axis=-1, keepdims=True) + EPS) * g


def _rope_lane_table(group, half):
    lane = np.arange(LANES)
    m = lane % group
    tab = np.zeros((SUBLANES, LANES), np.float32)
    tab[0] = np.where(m < 2 * half, (m % half) / half, 0.0)
    tab[1] = (m < half)
    tab[2] = (m >= half) & (m < 2 * half)
    tab[3] = (m < 2 * half)
    return tab


def _rope_coeffs(pos_f, tab_ref, out_ref):
    inv_freq = tab_ref[3:4, :] / jnp.power(jnp.float32(ROPE_THETA), tab_ref[0:1, :])
    ang = pos_f * inv_freq
    s = jnp.sin(ang)
    out_ref[0] = jnp.cos(ang)
    out_ref[1] = -s * tab_ref[1:2, :]
    out_ref[2] = s * tab_ref[2:3, :]


def _apply_rope(x, coef_ref, half):
    up = pltpu.roll(x, LANES - half, 1)
    dn = pltpu.roll(x, half, 1)
    return x * coef_ref[0] + up * coef_ref[1] + dn * coef_ref[2]


def _relbias_kernel(u_ref, o_ref):
    x = jnp.broadcast_to(u_ref[0, 0:1, :], (QA_BLOCK, 1024))
    y = pltpu.roll(x, 0, 1, stride=1, stride_axis=0)[:, :QA_WIN]
    qc = lax.broadcasted_iota(jnp.int32, (QA_BLOCK, QA_WIN), 0) // CHUNK
    kc = lax.broadcasted_iota(jnp.int32, (QA_BLOCK, QA_WIN), 1) // CHUNK
    band = (kc >= qc) & (kc <= qc + LEFT_CHUNKS)
    o_ref[0] = jnp.where(band, y, NEG)


def _relbias_block(rel_bias):
    p = np.arange(1024)
    off = np.where(p < QA_WIN, p, p - 1024)
    idx = np.clip(LEFT_CHUNKS * CHUNK - off, -MAX_REL, MAX_REL) + MAX_REL
    u = jnp.take(rel_bias.astype(F32), jnp.asarray(idx, jnp.int32), axis=1)
    u = jnp.broadcast_to(u[:, None, :], (N_HEADS, SUBLANES, 1024))
    return pl.pallas_call(
        _relbias_kernel,
        grid=(N_HEADS,),
        in_specs=[pl.BlockSpec((1, SUBLANES, 1024), lambda h: (h, 0, 0))],
        out_specs=pl.BlockSpec((1, QA_BLOCK, QA_WIN), lambda h: (h, 0, 0)),
        out_shape=jax.ShapeDtypeStruct((N_HEADS, QA_BLOCK, QA_WIN), F32),
        name="relbias",
    )(u)


R_QB, R_VA, R_KB, R_KA, R_QI, R_QA = range(6)


def _inproj_kernel(x_ref, g_ref, w0_ref, w1_ref, wvt_ref, wtail_ref, qkg_ref, pos_ref, tab128_ref, tab64_ref,
                   main_ref, vbt_ref, kiab_ref, wi_ref,
                   hn_ref, acc0_ref, acc1_ref, rope128_ref, rope64_ref):
    j = pl.program_id(1)
    tm = x_ref.shape[0]

    @pl.when(j == 0)
    def _():
        hn_ref[...] = _rms(x_ref[...], g_ref[...]).astype(BF16)
        pos_f = pos_ref[...].astype(F32)
        _rope_coeffs(pos_f, tab128_ref, rope128_ref)
        _rope_coeffs(pos_f, tab64_ref, rope64_ref)
        vt = _nt_dot(wvt_ref[...], hn_ref[...]).astype(BF16)
        for t in range(tm // TQ_DSA):
            vbt_ref[t] = vt[:, t * TQ_DSA:(t + 1) * TQ_DSA]
        tail = jnp.dot(hn_ref[...], wtail_ref[...], preferred_element_type=F32)
        kk = _apply_rope(tail[:, :LANES], rope64_ref, IDX_DIM // ROPE_FRACTION // 2)
        lane = lax.broadcasted_iota(jnp.int32, (tm, LANES), 1)
        kiab_ref[:, :LANES] = jnp.where(lane < IDX_DIM, kk, 0.0).astype(BF16)
        kiab_ref[:, LANES:] = jnp.where(lane >= IDX_DIM, kk, 0.0).astype(BF16)
        wi_ref[...] = tail[:, LANES:]

    half_b = HEAD_DIM // ROPE_FRACTION // 2
    half_i = IDX_DIM // ROPE_FRACTION // 2
    epilogues = {
        R_QA: lambda a: _rms(a, qkg_ref[0:1, :]),
        R_KA: lambda a: _rms(a, qkg_ref[1:2, :]),
        R_VA: lambda a: a,
        R_QB: lambda a: _apply_rope(_rms(a, qkg_ref[2:3, :]), rope128_ref, half_b),
        R_KB: lambda a: _apply_rope(_rms(a, qkg_ref[3:4, :]), rope128_ref, half_b),
        R_QI: lambda a: _apply_rope(a, rope64_ref, half_i),
    }

    def project_pair(first):
        accs = (acc0_ref, acc1_ref)
        for r, w_ref in enumerate((w0_ref, w1_ref)):
            accs[r][...] = jnp.dot(hn_ref[...], w_ref[...], preferred_element_type=F32)
        for r in range(2):
            fn = epilogues[first + r]
            for h in range(N_HEADS):
                sl = slice(h * HEAD_DIM, (h + 1) * HEAD_DIM)
                main_ref[:, r * WIDTH + h * HEAD_DIM:r * WIDTH + (h + 1) * HEAD_DIM] = fn(accs[r][:, sl]).astype(BF16)

    for pair in range(3):
        pl.when(j == pair)(functools.partial(project_pair, 2 * pair))


def _inproj(x2, pos2, g_mix, w_in, qk_a, qk_b):
    m = x2.shape[0]
    tm = TM_PROJ
    w = w_in.astype(BF16)
    o = [0, WIDTH, 2 * WIDTH, 3 * WIDTH, 4 * WIDTH, 5 * WIDTH, 6 * WIDTH, 7 * WIDTH]
    src = {R_QA: 0, R_KA: 1, R_VA: 2, R_QB: 3, R_KB: 4, R_QI: 6}

    def w_spec(r):
        blocks = [src[2 * pair + r] for pair in range(3)]
        return pl.BlockSpec((D_MODEL, WIDTH),
                            lambda i, j: (0, jnp.where(j == 0, blocks[0], jnp.where(j == 1, blocks[1], blocks[2]))))

    w_vt = w[:, o[5]:o[6]].T
    w_ki = w[:, o[7]:o[7] + IDX_DIM]
    w_wi = w[:, o[7] + IDX_DIM:o[7] + IDX_DIM + IDX_HEADS]
    w_tail = jnp.concatenate([w_ki, w_ki, w_wi, jnp.zeros((D_MODEL, LANES - IDX_HEADS), BF16)], axis=1)
    qkg = jnp.concatenate([qk_a, qk_b, jnp.zeros((4, HEAD_DIM), F32)], axis=0).astype(F32)
    tab128 = jnp.asarray(_rope_lane_table(HEAD_DIM, HEAD_DIM // ROPE_FRACTION // 2))
    tab64 = jnp.asarray(_rope_lane_table(IDX_DIM, IDX_DIM // ROPE_FRACTION // 2))
    res = lambda shape: pl.BlockSpec(shape, lambda i, j: (0,) * len(shape))
    return pl.pallas_call(
        _inproj_kernel,
        grid=(m // tm, 3),
        in_specs=[
            pl.BlockSpec((tm, D_MODEL), lambda i, j: (i, 0)),
            res((1, D_MODEL)),
            w_spec(0),
            w_spec(1),
            res((WIDTH, D_MODEL)),
            res((D_MODEL, 2 * LANES)),
            res((SUBLANES, HEAD_DIM)),
            pl.BlockSpec((tm, 1), lambda i, j: (i, 0)),
            res((SUBLANES, LANES)),
            res((SUBLANES, LANES)),
        ],
        out_specs=[
            pl.BlockSpec((tm, 2 * WIDTH), lambda i, j: (i, j)),
            pl.BlockSpec((tm // TQ_DSA, WIDTH, TQ_DSA), lambda i, j: (i, 0, 0)),
            pl.BlockSpec((tm, 2 * LANES), lambda i, j: (i, 0)),
            pl.BlockSpec((tm, LANES), lambda i, j: (i, 0)),
        ],
        out_shape=[
            jax.ShapeDtypeStruct((m, 6 * WIDTH), BF16),
            jax.ShapeDtypeStruct((m // TQ_DSA, WIDTH, TQ_DSA), BF16),
            jax.ShapeDtypeStruct((m, 2 * LANES), BF16),
            jax.ShapeDtypeStruct((m, LANES), F32),
        ],
        scratch_shapes=[
            pltpu.VMEM((tm, D_MODEL), BF16),
            pltpu.VMEM((tm, WIDTH), F32),
            pltpu.VMEM((tm, WIDTH), F32),
            pltpu.VMEM((3, tm, LANES), F32),
            pltpu.VMEM((3, tm, LANES), F32),
        ],
        compiler_params=pltpu.CompilerParams(
            dimension_semantics=("parallel", "arbitrary"), vmem_limit_bytes=VMEM_LIMIT),
        name="inproj",
    )(x2, g_mix, w, w, w_vt, w_tail, qkg, pos2, tab128, tab64)


def _mixa_kernel(q_ref, k0_ref, k1_ref, k2_ref, v0_ref, v1_ref, v2_ref, bias_ref, o_ref):
    qb = pl.program_id(1)
    scale = HEAD_DIM ** -0.5
    k_refs = (k0_ref, k1_ref, k2_ref)
    v_refs = (v0_ref, v1_ref, v2_ref)
    pen = (jnp.where(qb >= 2, 0.0, NEG).astype(F32), jnp.where(qb >= 1, 0.0, NEG).astype(F32), None)
    for h in range(N_HEADS):
        sl = slice(h * HEAD_DIM, (h + 1) * HEAD_DIM)
        q = q_ref[:, sl]
        s = []
        for m in range(3):
            sm = _nt_dot(q, k_refs[m][:, sl]) * scale + bias_ref[h, :, m * QA_BLOCK:(m + 1) * QA_BLOCK]
            if pen[m] is not None:
                sm = sm + pen[m]
            s.append(sm)
        mx = jnp.maximum(jnp.maximum(jnp.max(s[0], axis=-1, keepdims=True), jnp.max(s[1], axis=-1, keepdims=True)),
                         jnp.max(s[2], axis=-1, keepdims=True))
        p = [jnp.exp(sm - mx) for sm in s]
        den = (jnp.sum(p[0], axis=-1, keepdims=True) + jnp.sum(p[1], axis=-1, keepdims=True)
               + jnp.sum(p[2], axis=-1, keepdims=True))
        o = jnp.dot(p[0].astype(BF16), v_refs[0][:, sl], preferred_element_type=F32)
        o = o + jnp.dot(p[1].astype(BF16), v_refs[1][:, sl], preferred_element_type=F32)
        o = o + jnp.dot(p[2].astype(BF16), v_refs[2][:, sl], preferred_element_type=F32)
        o_ref[:, sl] = (o / den).astype(BF16)


def _mixa(main, bias, batch, seq):
    m = main.shape[0]
    nqb = seq // QA_BLOCK

    def kv_spec(region, back):
        return pl.BlockSpec((QA_BLOCK, WIDTH),
                            lambda b, i: (b * nqb + jnp.maximum(i - back, 0), region))

    return pl.pallas_call(
        _mixa_kernel,
        grid=(batch, nqb),
        in_specs=[
            pl.BlockSpec((QA_BLOCK, WIDTH), lambda b, i: (b * nqb + i, R_QA)),
            kv_spec(R_KA, 2), kv_spec(R_KA, 1), kv_spec(R_KA, 0),
            kv_spec(R_VA, 2), kv_spec(R_VA, 1), kv_spec(R_VA, 0),
            pl.BlockSpec((N_HEADS, QA_BLOCK, QA_WIN), lambda b, i: (0, 0, 0)),
        ],
        out_specs=pl.BlockSpec((QA_BLOCK, WIDTH), lambda b, i: (b * nqb + i, 0)),
        out_shape=jax.ShapeDtypeStruct((m, WIDTH), BF16),
        compiler_params=pltpu.CompilerParams(
            dimension_semantics=("parallel", "parallel"), vmem_limit_bytes=VMEM_LIMIT),
        name="mixa",
    )(main, main, main, main, main, main, main, bias)


def _key_to_f32(t):
    return pltpu.bitcast(t ^ ((t >> 31) & jnp.int32(0x7FFFFFFF)), F32)


def _dsa_kernel(topk, qb_ref, kb_ref, vbt_ref, qi_ref, kiab_ref, wi_ref, o_ref,
                sc_ref, bias_ref, acc_ref, m_ref, l_ref):
    i = pl.program_id(1)
    tq = TQ_DSA
    nkt = i + 1
    c2 = HEAD_DIM ** -0.5 * 1.4426950408889634

    w_t = jnp.transpose(wi_ref[...]) * (IDX_HEADS ** -0.5 * IDX_DIM ** -0.5)
    qchunk = (i * tq + lax.broadcasted_iota(jnp.int32, (1, tq), 1)) // CHUNK
    row = lax.broadcasted_iota(jnp.int32, (tq, tq), 0)

    def admissible(kt):
        return (kt * tq + row) // CHUNK <= qchunk

    def idx_body(kt, carry):
        r0 = pl.multiple_of(kt * tq, tq)
        ka = kiab_ref[pl.ds(r0, tq), :LANES]
        kb = kiab_ref[pl.ds(r0, tq), LANES:]
        acc = jnp.zeros((tq, tq), F32)
        for p in range(IDX_HEADS // 2):
            qp = qi_ref[:, p * LANES:(p + 1) * LANES]
            acc = acc + w_t[2 * p:2 * p + 1, :] * jnp.maximum(_nt_dot(ka, qp), 0.0)
            acc = acc + w_t[2 * p + 1:2 * p + 2, :] * jnp.maximum(_nt_dot(kb, qp), 0.0)
        sc_ref[pl.ds(r0, tq), :] = jnp.where(admissible(kt), acc, -jnp.inf)
        return carry

    lax.fori_loop(0, nkt, idx_body, 0)

    n_adm = (qchunk + 1) * CHUNK
    need = n_adm > topk

    def count_ge(c_f):
        def cnt_body(kt, c):
            r0 = pl.multiple_of(kt * tq, tq)
            ge = jnp.where(sc_ref[pl.ds(r0, tq), :] >= c_f, 1, 0).astype(jnp.int32)
            return c + jnp.sum(ge.reshape(tq // SUBLANES, SUBLANES, tq), axis=0)

        c8 = lax.fori_loop(0, nkt, cnt_body, jnp.zeros((SUBLANES, tq), jnp.int32))
        return jnp.sum(c8, axis=0, keepdims=True)

    def bit_cond(carry):
        b, _, cnt = carry
        todo = jnp.max(jnp.where(need & (cnt != topk), 1, 0))
        return (b >= 0) & (todo > 0)

    def bit_body(carry):
        b, thr, cnt = carry
        for _ in range(BITS_PER_CHECK):
            cand = thr + lax.shift_left(jnp.int32(1), b)
            n = count_ge(_key_to_f32(cand))
            ok = n >= topk
            b, thr, cnt = b - 1, jnp.where(ok, cand, thr), jnp.where(ok, n, cnt)
        return b, thr, cnt

    init = (jnp.int32(31), jnp.full((1, tq), jnp.iinfo(jnp.int32).min, jnp.int32),
            jnp.full((1, tq), jnp.iinfo(jnp.int32).max, jnp.int32))
    init = lax.fori_loop(0, UNCHECKED_BITS // BITS_PER_CHECK, lambda _, c: bit_body(c), init)
    _, thr, _ = lax.while_loop(bit_cond, bit_body, init)
    thr_f = _key_to_f32(thr)

    def mask_body(kt, carry):
        r0 = pl.multiple_of(kt * tq, tq)
        sel = admissible(kt) & ((sc_ref[pl.ds(r0, tq), :] >= thr_f) | jnp.logical_not(need))
        bias_ref[pl.ds(r0, tq), :] = jnp.where(sel, 0.0, NEG).astype(F32)
        return carry

    lax.fori_loop(0, nkt, mask_body, 0)

    acc_ref[...] = jnp.zeros_like(acc_ref)
    m_ref[...] = jnp.full(m_ref.shape, NEG, F32)
    l_ref[...] = jnp.zeros_like(l_ref)

    def att_body(kt, carry):
        for half in range(tq // KSUB_DSA):
            r0 = pl.multiple_of(kt * tq + half * KSUB_DSA, KSUB_DSA)
            ks = slice(half * KSUB_DSA, (half + 1) * KSUB_DSA)
            for h in range(N_HEADS):
                sl = slice(h * HEAD_DIM, (h + 1) * HEAD_DIM)
                s = _nt_dot(kb_ref[pl.ds(r0, KSUB_DSA), sl], qb_ref[:, sl]) + bias_ref[pl.ds(r0, KSUB_DSA), :]
                m_old = m_ref[h:h + 1, :]
                m_new = jnp.maximum(m_old, jnp.max(s, axis=0, keepdims=True))
                alpha = jnp.exp2((m_old - m_new) * c2)
                p = jnp.exp2((s - m_new) * c2)
                l_ref[h:h + 1, :] = alpha * l_ref[h:h + 1, :] + jnp.sum(p, axis=0, keepdims=True)
                m_ref[h:h + 1, :] = m_new
                pv = jnp.dot(vbt_ref[kt, sl, ks], p.astype(BF16), preferred_element_type=F32)
                acc_ref[h] = alpha * acc_ref[h] + pv
        return carry

    lax.fori_loop(0, nkt, att_body, 0)
    for h in range(N_HEADS):
        sl = slice(h * HEAD_DIM, (h + 1) * HEAD_DIM)
        o_ref[:, sl] = jnp.transpose(acc_ref[h] / l_ref[h:h + 1, :]).astype(BF16)


def _dsa(main, vbt, kiab, wi, batch, seq):
    m = main.shape[0]
    tq = TQ_DSA
    nq = seq // tq
    topk = min(TOPK_MAX, seq // 4)
    return pl.pallas_call(
        functools.partial(_dsa_kernel, topk),
        grid=(batch, nq),
        in_specs=[
            pl.BlockSpec((tq, WIDTH), lambda b, i: (b * nq + i, R_QB)),
            pl.BlockSpec((seq, WIDTH), lambda b, i: (b, R_KB)),
            pl.BlockSpec((nq, WIDTH, tq), lambda b, i: (b, 0, 0)),
            pl.BlockSpec((tq, WIDTH), lambda b, i: (b * nq + i, R_QI)),
            pl.BlockSpec((seq, 2 * LANES), lambda b, i: (b, 0)),
            pl.BlockSpec((tq, LANES), lambda b, i: (b * nq + i, 0)),
        ],
        out_specs=pl.BlockSpec((tq, WIDTH), lambda b, i: (b * nq + i, 0)),
        out_shape=jax.ShapeDtypeStruct((m, WIDTH), BF16),
        scratch_shapes=[pltpu.VMEM((seq, tq), F32), pltpu.VMEM((seq, tq), F32),
                        pltpu.VMEM((N_HEADS, HEAD_DIM, tq), F32),
                        pltpu.VMEM((N_HEADS, tq), F32), pltpu.VMEM((N_HEADS, tq), F32)],
        compiler_params=pltpu.CompilerParams(
            dimension_semantics=("parallel", "arbitrary"), vmem_limit_bytes=VMEM_LIMIT),
        name="dsa",
    )(main, main, vbt, main, kiab, wi)


def _outproj_kernel(oa_ref, ob_ref, x_ref, wo_ref, g_ref, h1_ref, xn_ref):
    acc = jnp.dot(oa_ref[...], wo_ref[:WIDTH, :], preferred_element_type=F32)
    acc = acc + jnp.dot(ob_ref[...], wo_ref[WIDTH:, :], preferred_element_type=F32)
    h1 = x_ref[...] + acc
    h1_ref[...] = h1
    xn_ref[...] = _rms(h1, g_ref[...]).astype(BF16)


def _outproj(out_a, out_b, x2, w_out, g_ffn):
    m = x2.shape[0]
    tm = TM_PROJ
    return pl.pallas_call(
        _outproj_kernel,
        grid=(m // tm,),
        in_specs=[
            pl.BlockSpec((tm, WIDTH), lambda i: (i, 0)),
            pl.BlockSpec((tm, WIDTH), lambda i: (i, 0)),
            pl.BlockSpec((tm, D_MODEL), lambda i: (i, 0)),
            pl.BlockSpec((2 * WIDTH, D_MODEL), lambda i: (0, 0)),
            pl.BlockSpec((1, D_MODEL), lambda i: (0, 0)),
        ],
        out_specs=[pl.BlockSpec((tm, D_MODEL), lambda i: (i, 0)), pl.BlockSpec((tm, D_MODEL), lambda i: (i, 0))],
        out_shape=[jax.ShapeDtypeStruct((m, D_MODEL), F32), jax.ShapeDtypeStruct((m, D_MODEL), BF16)],
        compiler_params=pltpu.CompilerParams(dimension_semantics=("parallel",), vmem_limit_bytes=VMEM_LIMIT),
        name="outproj",
    )(out_a, out_b, x2, w_out.astype(BF16), g_ffn)


def _merge_exchange_pairs(n):
    t = (n - 1).bit_length()
    pairs = []
    p = 1 << (t - 1)
    while p > 0:
        q, r, d = 1 << (t - 1), 0, p
        while True:
            pairs.extend((i, i + d) for i in range(n - d) if (i & p) == r)
            if q == p:
                break
            d, q, r = q - p, q >> 1, p
        p >>= 1
    return pairs


_SORT16 = _merge_exchange_pairs(PEER_TOPK)


def _tmax(a, b):
    if a is None:
        return b
    if b is None:
        return a
    return jnp.maximum(a, b)


def _cmp_exchange(x, i, j):
    a, b = x[i], x[j]
    if b is None:
        return
    if a is None:
        x[i], x[j] = b, None
        return
    x[i], x[j] = jnp.maximum(a, b), jnp.minimum(a, b)


def _sort16_desc(x):
    for i, j in _SORT16:
        _cmp_exchange(x, i, j)


def _bitonic16_desc(x):
    for d in (8, 4, 2, 1):
        for i in range(PEER_TOPK):
            if not i & d:
                _cmp_exchange(x, i, i + d)


def _merge_sublanes(x, shift, sort=True):
    y = [None if v is None else pltpu.roll(v, shift, 0) for v in x]
    out = [_tmax(x[i], y[PEER_TOPK - 1 - i]) for i in range(PEER_TOPK)]
    if sort:
        _bitonic16_desc(out)
    return out


def _top16(tiles):
    x = list(tiles)
    _sort16_desc(x)
    for shift in (4, 2, 1):
        x = _merge_sublanes(x, shift)
    return x


def _pair_threshold(v1, v2):
    sub = lax.broadcasted_iota(jnp.int32, v1[0].shape, 0)

    def diag(vals):
        d = vals[0]
        for s in range(1, SUBLANES):
            d = jnp.where(sub == s, vals[s], d)
        return d

    d1a, d1b = diag(v1[:8]), diag(v1[8:])
    d2a, d2b = diag(v2[:8]), diag(v2[8:])
    cand = [d1a + v2[0], d1b + v2[0], d1a + v2[1], d1a + v2[2], d1a + v2[3], v1[0] + d2b]
    cand += [jnp.where(sub < 4, -jnp.inf, v1[r] + d2a) for r in range(3)]
    x = cand + [None] * (PEER_TOPK - len(cand))
    _sort16_desc(x)
    x = _merge_sublanes(x, 4)
    x = _merge_sublanes(x, 2)
    x = _merge_sublanes(x, 1, sort=False)
    tau = x[0]
    for v in x[1:]:
        tau = jnp.minimum(tau, v)
    return tau


def _sublane_total(x):
    for shift in (4, 2, 1):
        x = x + pltpu.roll(x, shift, 0)
    return x


def _route_kernel(xn_ref, wq_ref, sk_ref, s2_ref, a2_ref, c_ref, a1_ref, qv_ref):
    tm = xn_ref.shape[0]
    n_t = PEER_KEYS // SUBLANES
    qv = jnp.dot(xn_ref[...], wq_ref[...], preferred_element_type=F32)
    for c in range(2 * PEER_HEADS):
        qv_ref[c] = qv[:, c * PEER_KEYS:(c + 1) * PEER_KEYS].astype(BF16)

    def chunk(h, lc):
        t0 = pl.multiple_of(lc * LANES, LANES)
        s1 = _nt_dot(sk_ref[0], qv_ref[2 * h, pl.ds(t0, LANES), :])
        s2 = _nt_dot(sk_ref[1], qv_ref[2 * h + 1, pl.ds(t0, LANES), :])
        s1 = [s1[j * SUBLANES:(j + 1) * SUBLANES, :] for j in range(n_t)]
        s2 = [s2[j * SUBLANES:(j + 1) * SUBLANES, :] for j in range(n_t)]
        v1 = _top16(s1)
        v2 = _top16(s2)
        tau = _pair_threshold(v1, v2)
        pre = []
        for q in range(PEER_TOPK):
            e = jnp.exp(v2[q] - v2[0])
            pre.append(e if q == 0 else pre[-1] + e)
        half_q = PEER_TOPK // 2
        c_top = jnp.full(v1[0].shape, jnp.inf, F32)
        m_top = jnp.zeros(v1[0].shape, F32)
        for q in range(half_q, PEER_TOPK):
            ok = v1[0] + v2[q] >= tau
            c_top = jnp.where(ok, v2[q], c_top)
            m_top = jnp.where(ok, pre[q], m_top)
        deep = c_top < jnp.inf
        a1, cth, z = [], [], None
        for j in range(n_t):
            c_j = jnp.full(s1[j].shape, jnp.inf, F32)
            m_j = jnp.zeros(s1[j].shape, F32)
            for q in range(half_q):
                ok = s1[j] + v2[q] >= tau
                c_j = jnp.where(ok, v2[q], c_j)
                m_j = jnp.where(ok, pre[q], m_j)
            top = (s1[j] == v1[0]) & deep
            c_j = jnp.where(top, c_top, c_j)
            m_j = jnp.where(top, m_top, m_j)
            a_j = jnp.exp(s1[j] - v1[0])
            a1.append(a_j)
            cth.append(c_j)
            z = a_j * m_j if z is None else z + a_j * m_j
        z = _sublane_total(z)
        for j in range(n_t):
            rs = slice(j * SUBLANES, (j + 1) * SUBLANES)
            s2_ref[h, lc, rs, :] = s2[j]
            a2_ref[h, lc, rs, :] = jnp.exp(s2[j] - v2[0])
            c_ref[h, lc, rs, :] = cth[j]
            a1_ref[h, lc, rs, :] = a1[j] / z

    def head_body(h, carry):
        def chunk_body(lc, c):
            chunk(h, lc)
            return c

        return lax.fori_loop(0, tm // LANES, chunk_body, carry)

    lax.fori_loop(0, PEER_HEADS, head_body, 0)


def _route(xn, w_query, sub_keys):
    m = xn.shape[0]
    tm = TM_ROUTE
    spec = pl.BlockSpec((PEER_HEADS, tm // LANES, PEER_KEYS, LANES), lambda i: (0, i, 0, 0))
    shp = jax.ShapeDtypeStruct((PEER_HEADS, m // LANES, PEER_KEYS, LANES), F32)
    return pl.pallas_call(
        _route_kernel,
        grid=(m // tm,),
        in_specs=[
            pl.BlockSpec((tm, D_MODEL), lambda i: (i, 0)),
            pl.BlockSpec((D_MODEL, 2 * PEER_HEADS * PEER_KEYS), lambda i: (0, 0)),
            pl.BlockSpec((2, PEER_KEYS, PEER_KEYS), lambda i: (0, 0, 0)),
        ],
        out_specs=[spec, spec, spec, spec],
        out_shape=[shp, shp, shp, shp],
        scratch_shapes=[pltpu.VMEM((2 * PEER_HEADS, tm, PEER_KEYS), BF16)],
        compiler_params=pltpu.CompilerParams(dimension_semantics=("parallel",), vmem_limit_bytes=VMEM_LIMIT),
        name="route",
    )(xn, w_query.astype(BF16), sub_keys.astype(BF16))


def _zero_row_from(x):
    bits = pltpu.bitcast(x, jnp.int32)
    z = lax.shift_right_logical(lax.shift_right_logical(bits, 16), 16)
    return z[0:1, :].astype(F32)


def _gate_piece(h_ref, coef_ref, p, e1_base, s2_ref, a2_ref, c_ref, a1_ref, anchors):
    n_e1 = TE_EXP // PEER_KEYS
    unit = 0
    for c in range(TP_EXP // LANES):
        lc = p * (TP_EXP // LANES) + c
        ls = slice(c * LANES, (c + 1) * LANES)
        for r0 in range(0, PEER_KEYS, GATE_ROWS):
            kr = slice(r0, r0 + GATE_ROWS)
            for e0 in range(0, n_e1, 2):
                gates = [jnp.zeros((GATE_ROWS, LANES), F32), jnp.zeros((GATE_ROWS, LANES), F32)]
                for h in range(PEER_HEADS):
                    anchor = anchors[(unit * PEER_HEADS + h) // ANCHOR_EVERY]
                    s2 = s2_ref[h, lc, kr, :]
                    a2 = a2_ref[h, lc, kr, :]
                    for k in range(2):
                        e1 = e1_base + e0 + k
                        cth = c_ref[h, lc, pl.ds(e1, 1), :] + anchor
                        a1 = a1_ref[h, lc, pl.ds(e1, 1), :]
                        gates[k] = gates[k] + jnp.where(s2 >= cth, a2, 0.0) * a1
                for k in range(2):
                    rs = slice((e0 + k) * PEER_KEYS + r0, (e0 + k) * PEER_KEYS + r0 + GATE_ROWS)
                    act = jax.nn.gelu(h_ref[p, rs, ls])
                    coef_ref[p, rs, ls] = (gates[k] * act).astype(BF16)
                unit += 1


def _experts_kernel(n_steps, xn_ref, u_ref, vt_ref, s2_ref, a2_ref, c_ref, a1_ref, o_ref,
                    h0_ref, h1_ref, c0_ref, c1_ref):
    g = pl.program_id(1)
    te = TE_EXP
    n_e1 = te // PEER_KEYS
    last_e1 = (2 * n_steps - 1) * n_e1
    n_pieces = xn_ref.shape[0] // TP_EXP
    n_anchor = (TP_EXP // LANES) * (PEER_KEYS // GATE_ROWS) * (n_e1 // 2) * PEER_HEADS // ANCHOR_EVERY

    @pl.when(g == 0)
    def _():
        o_ref[...] = jnp.zeros_like(o_ref)

    gates = (s2_ref, a2_ref, c_ref, a1_ref)

    def half_step(u_rows, v_cols, hb_ref, hc_ref, cc_ref, ca_ref, e1_base, do_b, do_c, do_a):
        def body(p, carry):
            t0 = pl.multiple_of(p * TP_EXP, TP_EXP)
            if do_b:
                hb_ref[p] = _nt_dot(u_ref[u_rows, :], xn_ref[pl.ds(t0, TP_EXP), :])
            if do_a:
                res = jnp.dot(vt_ref[:, v_cols], ca_ref[p], preferred_element_type=F32)
                o_ref[p] += res
            if do_c:
                if do_a:
                    step = (D_MODEL * ANCHOR_SPAN_PCT // 100) // n_anchor // SUBLANES * SUBLANES
                    anchors = [_zero_row_from(res[k * step:k * step + SUBLANES, 0:LANES]) for k in range(n_anchor)]
                else:
                    anchors = [jnp.zeros((1, LANES), F32)] * n_anchor
                _gate_piece(hc_ref, cc_ref, p, e1_base, *gates, anchors)
            return carry

        lax.fori_loop(0, n_pieces, body, 0)

    def step(first, second):
        half_step(slice(0, te), slice(0, te), h0_ref, h1_ref, c1_ref, c0_ref,
                  jnp.clip((2 * g - 1) * n_e1, 0, last_e1), *first)
        half_step(slice(te, 2 * te), slice(te, 2 * te), h1_ref, h0_ref, c0_ref, c1_ref,
                  jnp.minimum(2 * g * n_e1, last_e1), *second)

    pl.when(g == 0)(functools.partial(step, (True, False, False), (True, True, False)))
    pl.when((g > 0) & (g < n_steps))(functools.partial(step, (True, True, True), (True, True, True)))
    pl.when(g == n_steps)(functools.partial(step, (False, True, True), (False, False, True)))


def _experts(xn, expert_u, expert_v, s2, a2, cth, a1):
    m = xn.shape[0]
    tm, te, tp = TM_EXP, TE_EXP, TP_EXP
    n_exp = expert_u.shape[0]
    n_steps = n_exp // (2 * te)
    gspec = pl.BlockSpec((PEER_HEADS, tm // LANES, PEER_KEYS, LANES), lambda i, g: (0, i, 0, 0))
    return pl.pallas_call(
        functools.partial(_experts_kernel, n_steps),
        grid=(m // tm, n_steps + 1),
        in_specs=[
            pl.BlockSpec((tm, D_MODEL), lambda i, g: (i, 0)),
            pl.BlockSpec((2 * te, D_MODEL), lambda i, g: (jnp.minimum(g, n_steps - 1), 0)),
            pl.BlockSpec((D_MODEL, 2 * te), lambda i, g: (0, jnp.maximum(g - 1, 0))),
            gspec, gspec, gspec, gspec,
        ],
        out_specs=pl.BlockSpec((tm // tp, D_MODEL, tp), lambda i, g: (i, 0, 0)),
        out_shape=jax.ShapeDtypeStruct((m // tp, D_MODEL, tp), F32),
        scratch_shapes=[pltpu.VMEM((tm // tp, te, tp), F32), pltpu.VMEM((tm // tp, te, tp), F32),
                        pltpu.VMEM((tm // tp, te, tp), BF16), pltpu.VMEM((tm // tp, te, tp), BF16)],
        compiler_params=pltpu.CompilerParams(
            dimension_semantics=("parallel", "arbitrary"), vmem_limit_bytes=VMEM_LIMIT),
        name="experts",
    )(xn, expert_u.astype(BF16), expert_v.astype(BF16).T, s2, a2, cth, a1)


def _final_kernel(h1_ref, ot_ref, p_ref, g_ref, wg_ref, wp_ref, o_ref):
    h2 = h1_ref[...] + jnp.transpose(ot_ref[0])
    hn = _rms(h2, g_ref[...]).astype(BF16)
    gate = jax.nn.sigmoid(jnp.dot(hn, wg_ref[...], preferred_element_type=F32))
    pe = jnp.dot(p_ref[...].astype(BF16), wp_ref[...], preferred_element_type=F32)
    o_ref[...] = h2 + gate * pe


def _final(h1, out_t, p2, g_ple, w_gate, w_proj):
    m = h1.shape[0]
    tm = TM_FINAL
    assert out_t.shape == (m // tm, D_MODEL, tm)
    return pl.pallas_call(
        _final_kernel,
        grid=(m // tm,),
        in_specs=[
            pl.BlockSpec((tm, D_MODEL), lambda i: (i, 0)),
            pl.BlockSpec((1, D_MODEL, tm), lambda i: (i, 0, 0)),
            pl.BlockSpec((tm, PLE_DIM), lambda i: (i, 0)),
            pl.BlockSpec((1, D_MODEL), lambda i: (0, 0)),
            pl.BlockSpec((D_MODEL, D_MODEL), lambda i: (0, 0)),
            pl.BlockSpec((PLE_DIM, D_MODEL), lambda i: (0, 0)),
        ],
        out_specs=pl.BlockSpec((tm, D_MODEL), lambda i: (i, 0)),
        out_shape=jax.ShapeDtypeStruct((m, D_MODEL), F32),
        compiler_params=pltpu.CompilerParams(dimension_semantics=("parallel",), vmem_limit_bytes=VMEM_LIMIT),
        name="final",
    )(h1, out_t, p2, g_ple, w_gate.astype(BF16), w_proj.astype(BF16))


def _layer(h2d, p2d, pos2, batch, seq, norm_mix, w_in, qk_a, qk_b, rel_bias, w_out, norm_ffn,
           peer_query, sub_keys, peer_u, peer_v, norm_ple, ple_gate, ple_proj):
    bias = _relbias_block(rel_bias)
    main, vbt, kiab, wi = _inproj(h2d, pos2, norm_mix[None, :], w_in, qk_a, qk_b)
    out_a = _mixa(main, bias, batch, seq)
    out_b = _dsa(main, vbt, kiab, wi, batch, seq)
    h1, xn = _outproj(out_a, out_b, h2d, w_out, norm_ffn[None, :])
    s2, a2, cth, a1 = _route(xn, peer_query, sub_keys)
    out_t = _experts(xn, peer_u, peer_v, s2, a2, cth, a1)
    return _final(h1, out_t, p2d, norm_ple[None, :], ple_gate, ple_proj)


def kernel(x, p, positions, norm_mix, w_in, qk_norm_a, qk_norm_b, rel_bias, w_out, norm_ffn, peer_query,
           peer_sub_keys, peer_u, peer_v, norm_ple, ple_gate, ple_proj):
    batch, seq, d = x.shape
    assert d == D_MODEL and seq % TM_PROJ == 0 and (batch * seq) % TM_PROJ == 0
    h = x.reshape(batch * seq, d)
    pos2 = positions.reshape(batch * seq, 1).astype(jnp.int32)
    for i in range(p.shape[0]):
        h = _layer(h, p[i].reshape(batch * seq, PLE_DIM), pos2, batch, seq, norm_mix[i], w_in[i],
                   qk_norm_a[i], qk_norm_b[i], rel_bias[i], w_out[i], norm_ffn[i], peer_query[i],
                   peer_sub_keys[i], peer_u[i], peer_v[i], norm_ple[i], ple_gate[i], ple_proj[i])
    return h.reshape(batch, seq, d)
```
